```python
import jax, jax.numpy as jnp
from jax import lax
import numpy as np

D_MODEL = 1024
BATCH = 16
SEQ = 2048
DEPTH = 1
DEC_BATCH = 2
DEC_SEQ = 8192
PAST_LEN = 128

N_ATT_HEADS = 16
N_KV_GROUPS = 4
ATT_HEAD_DIM = D_MODEL // N_ATT_HEADS
ATT_WIDTH = N_ATT_HEADS * ATT_HEAD_DIM
KV_WIDTH = N_KV_GROUPS * ATT_HEAD_DIM
WINDOW = 128
BLOCK = 128
N_MLSTM_HEADS = 4
MLSTM_WIDTH = D_MODEL
MLSTM_HEAD_DIM = MLSTM_WIDTH // N_MLSTM_HEADS
CHUNK = 128
FORGET_BIAS = 3.0
D_FF = 2816
CONV_WIDTH = 3
EPS = 1e-6

SPLIT_SIZES = (ATT_WIDTH, KV_WIDTH, KV_WIDTH,
               MLSTM_WIDTH, MLSTM_WIDTH, MLSTM_WIDTH, MLSTM_WIDTH,
               2 * N_MLSTM_HEADS, 2 * N_MLSTM_HEADS, 2 * D_MODEL)
IN_WIDTH = int(sum(SPLIT_SIZES))
SPLIT_POINTS = tuple(int(c) for c in np.cumsum(SPLIT_SIZES)[:-1])

kernel_name = "hybrid_bidir_mlstm_swa_encoder"

F32 = jnp.float32


def rmsnorm(x, w):
    xf = x.astype(F32)
    y = xf * lax.rsqrt(jnp.mean(xf * xf, axis=-1, keepdims=True) + EPS)
    return (y * w.astype(F32)).astype(x.dtype)


def alibi_slopes(n_heads):
    return 2.0 ** (-8.0 * jnp.arange(1, n_heads + 1, dtype=F32) / n_heads)


def windowed_gqa_attention(q, k, v, sink):
    B, S, H, hd = q.shape
    G = k.shape[2]
    R = H // G
    nb = S // BLOCK
    qb = q.reshape(B, nb, BLOCK, G, R, hd)

    def windows(t):
        tb = t.reshape(B, nb, BLOCK, G, hd)
        pad = jnp.zeros((B, 1, BLOCK, G, hd), t.dtype)
        tp = jnp.concatenate([pad, tb, pad], axis=1)
        return jnp.concatenate([tp[:, :-2], tp[:, 1:-1], tp[:, 2:]], axis=2)

    kw = windows(k)
    vw = windows(v)
    rel = (jnp.arange(3 * BLOCK)[None, :] - BLOCK) - jnp.arange(BLOCK)[:, None]
    in_window = jnp.abs(rel) <= WINDOW
    alibi = (-alibi_slopes(H)[:, None, None] * jnp.abs(rel).astype(F32)[None]).reshape(G, R, BLOCK, 3 * BLOCK)
    sink_g = sink.astype(F32).reshape(G, R)
    scale = hd ** -0.5

    def block_fn(args):
        j, qj, kj, vj = args
        kpos = j * BLOCK - BLOCK + jnp.arange(3 * BLOCK)
        valid = in_window & ((kpos >= 0) & (kpos < S))[None, :]
        s = jnp.einsum('bqgrd,bkgd->bgrqk', qj.astype(F32), kj.astype(F32)) * scale + alibi
        s = jnp.where(valid, s, -jnp.inf)
        sink_col = jnp.broadcast_to(sink_g[None, :, :, None, None], s.shape[:-1] + (1,))
        p = jax.nn.softmax(jnp.concatenate([s, sink_col], axis=-1), axis=-1)[..., :-1]
        o = jnp.einsum('bgrqk,bkgd->bqgrd', p, vj.astype(F32))
        return o.astype(qj.dtype)

    out = lax.map(block_fn, (jnp.arange(nb), jnp.moveaxis(qb, 1, 0),
                             jnp.moveaxis(kw, 1, 0), jnp.moveaxis(vw, 1, 0)))
    return jnp.moveaxis(out, 0, 1).reshape(B, S, H * hd)


def mlstm_chunkwise(q, k, v, log_i, log_f):
    B, H, S, dh = q.shape
    nc = S // CHUNK

    def to_chunks(t):
        return jnp.moveaxis(t.reshape((B, H, nc, CHUNK) + t.shape[3:]), 2, 0)

    lower = jnp.tril(jnp.ones((CHUNK, CHUNK), dtype=bool))

    def step(carry, xs):
        C, n, m = carry
        qc, kc, vc, li, lf = xs
        b = jnp.cumsum(lf, axis=-1)
        d = b[..., :, None] - b[..., None, :] + li[..., None, :]
        d = jnp.where(lower, d, -jnp.inf)
        m_inter = b + m[..., None]
        m_t = jnp.maximum(m_inter, jnp.max(d, axis=-1))
        w_inter = jnp.exp(m_inter - m_t)
        a = jnp.einsum('bhtd,bhsd->bhts', qc, kc) * jnp.exp(d - m_t[..., None])
        num = w_inter[..., None] * jnp.einsum('bhtd,bhde->bhte', qc, C) + jnp.einsum('bhts,bhse->bhte', a, vc)
        den = w_inter * jnp.einsum('bhtd,bhd->bht', qc, n) + jnp.sum(a, axis=-1)
        h = num / jnp.maximum(jnp.abs(den), jnp.exp(-m_t))[..., None]
        b_last = b[..., -1]
        g = b_last[..., None] - b + li
        m_new = jnp.maximum(b_last + m, jnp.max(g, axis=-1))
        decay = jnp.exp(b_last + m - m_new)
        wk = jnp.exp(g - m_new[..., None])[..., None] * kc
        C_new = decay[..., None, None] * C + jnp.einsum('bhsd,bhse->bhde', wk, vc)
        n_new = decay[..., None] * n + jnp.sum(wk, axis=-2)
        return (C_new, n_new, m_new), h

    init = (jnp.zeros((B, H, dh, dh), F32), jnp.zeros((B, H, dh), F32), jnp.zeros((B, H), F32))
    _, hs = lax.scan(step, init, (to_chunks(q), to_chunks(k), to_chunks(v),
                                  to_chunks(log_i), to_chunks(log_f)))
    return jnp.moveaxis(hs, 0, 2).reshape(B, H, S, dh)


def bidirectional_mlstm(mq, mk, mv, mo, mi, mf, norm_w):
    B, S, _ = mq.shape
    Hm, dh = N_MLSTM_HEADS, MLSTM_HEAD_DIM

    def heads(t):
        return jnp.transpose(t.astype(F32).reshape(B, S, Hm, dh), (0, 2, 1, 3))

    q = heads(mq)
    k = heads(mk) * (dh ** -0.5)
    v = heads(mv)
    li = jnp.transpose(mi.astype(F32).reshape(B, S, 2, Hm), (2, 0, 3, 1))
    lf = jax.nn.log_sigmoid(jnp.transpose(mf.astype(F32).reshape(B, S, 2, Hm), (2, 0, 3, 1)) + FORGET_BIAS)
    h_fwd = mlstm_chunkwise(q, k, v, li[0], lf[0])
    flip = lambda t: jnp.flip(t, axis=2)
    h_bwd = flip(mlstm_chunkwise(flip(q), flip(k), flip(v), jnp.flip(li[1], -1), jnp.flip(lf[1], -1)))
    h = jax.nn.sigmoid(heads(mo)) * (h_fwd + h_bwd)
    h = h * lax.rsqrt(jnp.mean(h * h, axis=-1, keepdims=True) + EPS)
    h = jnp.transpose(h, (0, 2, 1, 3)).reshape(B, S, Hm * dh) * norm_w.astype(F32)
    return h.astype(mq.dtype)


def centred_depthwise_conv(a, w, b):
    S = a.shape[1]
    half = CONV_WIDTH // 2
    ap = jnp.pad(a, ((0, 0), (half, half), (0, 0)))
    out = sum(ap[:, t:t + S] * w[t] for t in range(CONV_WIDTH))
    return out + b


def encoder_layer(x, pre_mix_norm, w_in, b_in, attn_sink, mlstm_norm, w_att_branch,
                  w_mlstm_branch, w_out, post_mix_norm, pre_ffn_norm, w_ffn_up,
                  ffn_conv_w, ffn_conv_b, w_ffn_down, post_ffn_norm):
    B, S, _ = x.shape
    h = rmsnorm(x, pre_mix_norm)
    z = h @ w_in + b_in
    aq, ak, av, mq, mk, mv, mo, mi, mf, gates = jnp.split(z, SPLIT_POINTS, axis=-1)
    ya = windowed_gqa_attention(aq.reshape(B, S, N_ATT_HEADS, ATT_HEAD_DIM),
                                ak.reshape(B, S, N_KV_GROUPS, ATT_HEAD_DIM),
                                av.reshape(B, S, N_KV_GROUPS, ATT_HEAD_DIM), attn_sink)
    ym = bidirectional_mlstm(mq, mk, mv, mo, mi, mf, mlstm_norm)
    g_att, g_mlstm = jnp.split(jax.nn.sigmoid(gates), 2, axis=-1)
    merged = g_att * (ya @ w_att_branch) + g_mlstm * (ym @ w_mlstm_branch)
    x = x + rmsnorm(merged @ w_out, post_mix_norm)
    h2 = rmsnorm(x, pre_ffn_norm)
    a, gate = jnp.split(h2 @ w_ffn_up, 2, axis=-1)
    a = centred_depthwise_conv(a, ffn_conv_w, ffn_conv_b)
    f = jax.nn.gelu(a, approximate=True) * gate
    x = x + rmsnorm(f @ w_ffn_down, post_ffn_norm)
    return x


def setup_inputs(seed: int = 0) -> dict:
    key = jax.random.key(seed)
    ks = jax.random.split(key, 20)
    nrm = lambda k, shape, s: jax.random.normal(k, shape, F32) * s
    gain = lambda k, n: 1.0 + nrm(k, (DEPTH, n), 0.05)
    return {
        "x_prompt": nrm(ks[0], (BATCH, SEQ, D_MODEL), 1.0),
        "x_sample": nrm(ks[1], (DEC_BATCH, DEC_SEQ, D_MODEL), 1.0),
        "pre_mix_norm": gain(ks[2], D_MODEL),
        "w_in": nrm(ks[3], (DEPTH, D_MODEL, IN_WIDTH), D_MODEL ** -0.5),
        "b_in": nrm(ks[4], (DEPTH, IN_WIDTH), 0.02),
        "attn_sink": nrm(ks[5], (DEPTH, N_ATT_HEADS), 0.5),
        "mlstm_norm": gain(ks[6], MLSTM_WIDTH),
        "w_att_branch": nrm(ks[7], (DEPTH, ATT_WIDTH, D_MODEL), ATT_WIDTH ** -0.5),
        "w_mlstm_branch": nrm(ks[8], (DEPTH, MLSTM_WIDTH, D_MODEL), MLSTM_WIDTH ** -0.5),
        "w_out": nrm(ks[9], (DEPTH, D_MODEL, D_MODEL), D_MODEL ** -0.5),
        "post_mix_norm": gain(ks[10], D_MODEL),
        "pre_ffn_norm": gain(ks[11], D_MODEL),
        "w_ffn_up": nrm(ks[12], (DEPTH, D_MODEL, 2 * D_FF), D_MODEL ** -0.5),
        "ffn_conv_w": nrm(ks[13], (DEPTH, CONV_WIDTH, D_FF), CONV_WIDTH ** -0.5),
        "ffn_conv_b": nrm(ks[14], (DEPTH, D_FF), 0.02),
        "w_ffn_down": nrm(ks[15], (DEPTH, D_FF, D_MODEL), D_FF ** -0.5),
        "post_ffn_norm": gain(ks[16], D_MODEL),
    }


def reference(x_prompt, x_sample, pre_mix_norm, w_in, b_in, attn_sink, mlstm_norm,
              w_att_branch, w_mlstm_branch, w_out, post_mix_norm, pre_ffn_norm,
              w_ffn_up, ffn_conv_w, ffn_conv_b, w_ffn_down, post_ffn_norm):
    def trunk(x):
        for l in range(DEPTH):
            x = encoder_layer(x, pre_mix_norm[l], w_in[l], b_in[l], attn_sink[l], mlstm_norm[l],
                              w_att_branch[l], w_mlstm_branch[l], w_out[l], post_mix_norm[l],
                              pre_ffn_norm[l], w_ffn_up[l], ffn_conv_w[l], ffn_conv_b[l],
                              w_ffn_down[l], post_ffn_norm[l])
        return x

    y_prompt = trunk(x_prompt)
    y_sample = trunk(x_sample)
    return (y_prompt, y_sample)
```

```python
import functools

import jax
import jax.numpy as jnp
from jax import lax
from jax.experimental import pallas as pl
from jax.experimental.pallas import tpu as pltpu

F32 = jnp.float32
BF16 = jnp.bfloat16

D_MODEL = 1024
N_ATT_HEADS = 16
N_KV_GROUPS = 4
ATT_HEAD_DIM = 64
HEADS_PER_GROUP = N_ATT_HEADS // N_KV_GROUPS
KV_WIDTH = N_KV_GROUPS * ATT_HEAD_DIM
WINDOW = 128
BLOCK = 128
N_MLSTM_HEADS = 4
MLSTM_HEAD_DIM = 256
CHUNK = 128
FORGET_BIAS = 3.0
D_FF = 2816
EPS = 1e-6
MASK_VALUE = -1e30

LANES = 128
GATE_SLAB = LANES
VMEM_LIMIT_BYTES = 56 * 1024 * 1024

_SLABS = {}
_off = 0
for _name, _width in (("aq", 1024), ("ak", 256), ("av", 256), ("mq", 1024), ("mk", 1024),
                      ("mv", 1024), ("mo", 1024), ("gl", GATE_SLAB), ("ga", 1024), ("gm", 1024)):
    _SLABS[_name] = (_off, _width)
    _off += _width
PACKED_IN_WIDTH = _off

ROW_TILE = 512
FF_CHUNK = 256


def _const_spec(shape):
    nd = len(shape)
    return pl.BlockSpec(shape, lambda *_: (0,) * nd, pipeline_mode=pl.Buffered(1))


def _rms(x, w):
    return x * lax.rsqrt(jnp.mean(x * x, axis=-1, keepdims=True) + EPS) * w


def _inproj_kernel(x_ref, nw_ref, w_ref, b_ref, aq_ref, ak_ref, av_ref, mq_ref, mk_ref,
                   mv_ref, mo_ref, gl_ref, ga_ref, gm_ref):
    h = _rms(x_ref[...], nw_ref[...]).astype(BF16)

    def proj(name):
        s, w = _SLABS[name]
        return jnp.dot(h, w_ref[:, s:s + w], preferred_element_type=F32) + b_ref[:, s:s + w]

    aq_ref[...] = (proj("aq") * (ATT_HEAD_DIM ** -0.5)).astype(BF16)
    ak_ref[...] = proj("ak").astype(BF16)
    av_ref[...] = proj("av").astype(BF16)
    mq_ref[...] = proj("mq").astype(BF16)
    mk_ref[...] = (proj("mk") * (MLSTM_HEAD_DIM ** -0.5)).astype(BF16)
    mv_ref[...] = proj("mv").astype(BF16)
    mo_ref[...] = proj("mo").astype(BF16)
    gl_ref[...] = proj("gl")
    ga_ref[...] = jax.nn.sigmoid(proj("ga")).astype(BF16)
    gm_ref[...] = jax.nn.sigmoid(proj("gm")).astype(BF16)


def _inproj(x2d, norm_w, w_packed, b_packed):
    T = x2d.shape[0]
    tm = min(ROW_TILE, T)
    assert T % tm == 0
    row = lambda w: pl.BlockSpec((tm, w), lambda i: (i, 0))
    names = ("aq", "ak", "av", "mq", "mk", "mv", "mo", "gl", "ga", "gm")
    out_shape = tuple(jax.ShapeDtypeStruct((T, _SLABS[n][1]), F32 if n == "gl" else BF16)
                      for n in names)
    return pl.pallas_call(
        _inproj_kernel,
        grid=(T // tm,),
        in_specs=[row(D_MODEL), _const_spec((1, D_MODEL)),
                  _const_spec((D_MODEL, PACKED_IN_WIDTH)), _const_spec((1, PACKED_IN_WIDTH))],
        out_specs=tuple(row(_SLABS[n][1]) for n in names),
        out_shape=out_shape,
        compiler_params=pltpu.CompilerParams(dimension_semantics=("parallel",),
                                             vmem_limit_bytes=VMEM_LIMIT_BYTES),
        name="inproj",
    )(x2d, norm_w, w_packed, b_packed)


def _attn_kernel(sink_ref, q_ref, kp_ref, ko_ref, kn_ref, vp_ref, vo_ref, vn_ref, bias_ref,
                 o_ref):
    hd = ATT_HEAD_DIM
    for g in range(N_KV_GROUPS):
        ks = slice(g * hd, (g + 1) * hd)
        kg = jnp.concatenate([kp_ref[0, :, ks], ko_ref[0, :, ks], kn_ref[0, :, ks]], axis=0)
        vg = jnp.concatenate([vp_ref[0, :, ks], vo_ref[0, :, ks], vn_ref[0, :, ks]], axis=0)
        heads = [g * HEADS_PER_GROUP + r for r in range(HEADS_PER_GROUP)]
        qs = jnp.concatenate([q_ref[0, :, h * hd:(h + 1) * hd] for h in heads], axis=0)
        s = lax.dot_general(qs, kg, (((1,), (1,)), ((), ())), preferred_element_type=F32)
        ps, ls = [], []
        for r, h in enumerate(heads):
            sh = s[r * BLOCK:(r + 1) * BLOCK] + bias_ref[0, h]
            sink = sink_ref[h]
            m = jnp.maximum(jnp.max(sh, axis=-1, keepdims=True), sink)
            p = jnp.exp(sh - m)
            ls.append(jnp.sum(p, axis=-1, keepdims=True) + jnp.exp(sink - m))
            ps.append(p.astype(BF16))
        o = jnp.dot(jnp.concatenate(ps, axis=0), vg, preferred_element_type=F32)
        outs = [o[r * BLOCK:(r + 1) * BLOCK] / ls[r] for r in range(HEADS_PER_GROUP)]
        gs = slice(g * HEADS_PER_GROUP * hd, (g + 1) * HEADS_PER_GROUP * hd)
        o_ref[0, :, gs] = jnp.concatenate(outs, axis=1).astype(BF16)


def _attention_bias():
    rel = (jnp.arange(3 * BLOCK)[None, :] - BLOCK) - jnp.arange(BLOCK)[:, None]
    slopes = 2.0 ** (-8.0 * jnp.arange(1, N_ATT_HEADS + 1, dtype=F32) / N_ATT_HEADS)
    alibi = -slopes[:, None, None] * jnp.abs(rel).astype(F32)[None]
    band = (jnp.abs(rel) <= WINDOW)[None]
    col = jnp.arange(3 * BLOCK)[None, None, :]
    variants = []
    for has_prev, has_next in ((False, True), (True, True), (True, False)):
        ok = band & ((col >= BLOCK) | has_prev) & ((col < 2 * BLOCK) | has_next)
        variants.append(jnp.where(ok, alibi, MASK_VALUE))
    return jnp.stack(variants)


def _attention(aq, ak, av, sink, B, S):
    nb = S // BLOCK
    assert S % BLOCK == 0 and nb >= 2
    q3 = aq.reshape(B, S, N_ATT_HEADS * ATT_HEAD_DIM)
    k3 = ak.reshape(B, S, KV_WIDTH)
    v3 = av.reshape(B, S, KV_WIDTH)
    kv = lambda f: pl.BlockSpec((1, BLOCK, KV_WIDTH), lambda j, b: (b, f(j), 0))
    prev_, own_, next_ = (lambda j: jnp.maximum(j - 1, 0)), (lambda j: j), (
        lambda j: jnp.minimum(j + 1, nb - 1))
    variant = lambda j: jnp.where(j == 0, 0, jnp.where(j == nb - 1, 2, 1))
    return pl.pallas_call(
        _attn_kernel,
        grid=(nb, B),
        in_specs=[pl.BlockSpec(memory_space=pltpu.SMEM),
                  pl.BlockSpec((1, BLOCK, D_MODEL), lambda j, b: (b, j, 0)),
                  kv(prev_), kv(own_), kv(next_), kv(prev_), kv(own_), kv(next_),
                  pl.BlockSpec((1, N_ATT_HEADS, BLOCK, 3 * BLOCK),
                               lambda j, b: (variant(j), 0, 0, 0))],
        out_specs=pl.BlockSpec((1, BLOCK, D_MODEL), lambda j, b: (b, j, 0)),
        out_shape=jax.ShapeDtypeStruct((B, S, D_MODEL), BF16),
        compiler_params=pltpu.CompilerParams(dimension_semantics=("arbitrary", "arbitrary"),
                                             vmem_limit_bytes=VMEM_LIMIT_BYTES),
        name="attention",
    )(sink, q3, k3, k3, k3, v3, v3, v3, _attention_bias())


def _mlstm_kernel(*refs, reverse, final):
    if final:
        (q_ref, k_ref, v_ref, g_ref, hf_ref, mo_ref, nw_ref, out_ref, c_s, n_s, m_s) = refs
    else:
        (q_ref, k_ref, v_ref, g_ref, out_ref, c_s, n_s, m_s) = refs
    L, dh = CHUNK, MLSTM_HEAD_DIM
    direction = 1 if reverse else 0

    @pl.when(pl.program_id(1) == 0)
    def _():
        c_s[...] = jnp.zeros_like(c_s)
        n_s[...] = jnp.zeros_like(n_s)
        m_s[...] = jnp.zeros_like(m_s)

    gl = g_ref[0]
    lf_all = jax.nn.log_sigmoid(gl + FORGET_BIAS)
    t_idx = lax.broadcasted_iota(jnp.int32, (L, L), 0)
    s_idx = lax.broadcasted_iota(jnp.int32, (L, L), 1)
    seen = (s_idx >= t_idx) if reverse else (s_idx <= t_idx)
    bcol_all = jnp.dot(seen.astype(F32), lf_all, preferred_element_type=F32,
                       precision=lax.Precision.HIGHEST)
    brow_all = bcol_all.T
    lirow_all = gl.T
    last = 0 if reverse else L - 1

    for h in range(N_MLSTM_HEADS):
        ci = direction * N_MLSTM_HEADS + h
        cf = 2 * N_MLSTM_HEADS + ci
        hs = slice(h * dh, (h + 1) * dh)
        b_col, b_row = bcol_all[:, cf:cf + 1], brow_all[cf:cf + 1, :]
        li_col, li_row = gl[:, ci:ci + 1], lirow_all[ci:ci + 1, :]
        m_prev = m_s[h][:, 0:1]
        c_prev = c_s[h]
        n_prev = n_s[h]
        qh, kh, vh = q_ref[0, :, hs], k_ref[0, :, hs], v_ref[0, :, hs]

        d = jnp.where(seen, b_col - b_row + li_row, -jnp.inf)
        m_inter = b_col + m_prev
        m_t = jnp.maximum(m_inter, jnp.max(d, axis=-1, keepdims=True))
        w_inter = jnp.exp(m_inter - m_t)
        qk = lax.dot_general(qh, kh, (((1,), (1,)), ((), ())), preferred_element_type=F32)
        a = qk * jnp.exp(d - m_t)
        num = (w_inter * jnp.dot(qh, c_prev.astype(BF16), preferred_element_type=F32)
               + jnp.dot(a.astype(BF16), vh, preferred_element_type=F32))
        den = (w_inter * jnp.sum(qh.astype(F32) * n_prev, axis=-1, keepdims=True)
               + jnp.sum(a, axis=-1, keepdims=True))
        hh = num / jnp.maximum(jnp.abs(den), jnp.exp(-m_t))

        b_last = b_col[last:last + 1, :]
        g_col = b_last - b_col + li_col
        m_new = jnp.maximum(b_last + m_prev, jnp.max(g_col, axis=0, keepdims=True))
        decay = jnp.exp(b_last + m_prev - m_new)
        wk = jnp.exp(g_col - m_new) * kh.astype(F32)
        c_s[h] = decay * c_prev + lax.dot_general(
            wk.astype(BF16), vh, (((0,), (0,)), ((), ())), preferred_element_type=F32)
        n_s[h] = decay * n_prev + jnp.sum(wk, axis=0, keepdims=True)
        m_s[h] = jnp.broadcast_to(m_new, (1, LANES))

        if final:
            o = jax.nn.sigmoid(mo_ref[0, :, hs].astype(F32)) * (hf_ref[0, :, hs] + hh)
            out_ref[0, :, hs] = _rms(o, nw_ref[:, hs]).astype(out_ref.dtype)
        else:
            out_ref[0, :, hs] = hh


def _mlstm_scan(q3, k3, v3, g3, *, reverse, extra=()):
    B, S, W = q3.shape
    nc = S // CHUNK
    assert S % CHUNK == 0
    final = bool(extra)
    chunk = (lambda c: nc - 1 - c) if reverse else (lambda c: c)
    seq = lambda w: pl.BlockSpec((1, CHUNK, w), lambda b, c: (b, chunk(c), 0))
    in_specs = [seq(W), seq(W), seq(W), seq(GATE_SLAB)]
    if final:
        in_specs += [seq(W), seq(W), _const_spec((1, W))]
    return pl.pallas_call(
        functools.partial(_mlstm_kernel, reverse=reverse, final=final),
        grid=(B, nc),
        in_specs=in_specs,
        out_specs=seq(W),
        out_shape=jax.ShapeDtypeStruct((B, S, W), BF16 if final else F32),
        scratch_shapes=[pltpu.VMEM((N_MLSTM_HEADS, MLSTM_HEAD_DIM, MLSTM_HEAD_DIM), F32),
                        pltpu.VMEM((N_MLSTM_HEADS, 1, MLSTM_HEAD_DIM), F32),
                        pltpu.VMEM((N_MLSTM_HEADS, 1, LANES), F32)],
        compiler_params=pltpu.CompilerParams(dimension_semantics=("arbitrary", "arbitrary"),
                                             vmem_limit_bytes=VMEM_LIMIT_BYTES),
        name="mlstm_bwd_final" if final else "mlstm_fwd",
    )(q3, k3, v3, g3, *extra)


def _merge_kernel(x_ref, ya_ref, ym_ref, ga_ref, gm_ref, wa_ref, wm_ref, wo_ref, pmw_ref,
                  pfw_ref, x1_ref, h2_ref):
    merged = (ga_ref[...].astype(F32) * jnp.dot(ya_ref[...], wa_ref[...],
                                                 preferred_element_type=F32)
              + gm_ref[...].astype(F32) * jnp.dot(ym_ref[...], wm_ref[...],
                                                   preferred_element_type=F32))
    mixed = jnp.dot(merged.astype(BF16), wo_ref[...], preferred_element_type=F32)
    x1 = x_ref[...] + _rms(mixed, pmw_ref[...])
    x1_ref[...] = x1
    h2_ref[...] = _rms(x1, pfw_ref[...]).astype(BF16)


def _merge(x2d, ya, ym, ga, gm, wa, wm, wo, pmw, pfw):
    T = x2d.shape[0]
    tm = min(ROW_TILE, T)
    assert T % tm == 0
    row = pl.BlockSpec((tm, D_MODEL), lambda i: (i, 0))
    sq = _const_spec((D_MODEL, D_MODEL))
    vec = _const_spec((1, D_MODEL))
    return pl.pallas_call(
        _merge_kernel,
        grid=(T // tm,),
        in_specs=[row, row, row, row, row, sq, sq, sq, vec, vec],
        out_specs=(row, row),
        out_shape=(jax.ShapeDtypeStruct((T, D_MODEL), F32),
                   jax.ShapeDtypeStruct((T, D_MODEL), BF16)),
        compiler_params=pltpu.CompilerParams(dimension_semantics=("parallel",),
                                             vmem_limit_bytes=VMEM_LIMIT_BYTES),
        name="merge",
    )(x2d, ya, ym, ga, gm, wa, wm, wo, pmw, pfw)


HALO = 8


def _ffn_kernel(x1_ref, h2_ref, hp_ref, hn_ref, wu_ref, cw_ref, cb_ref, wd_ref, nw_ref,
                out_ref, f_s, *, tiles_per_seq):
    i = pl.program_id(0)
    tm = h2_ref.shape[0]
    has_prev = (i % tiles_per_seq) != 0
    has_next = (i % tiles_per_seq) != tiles_per_seq - 1
    h2 = h2_ref[...]
    h_prev = hp_ref[HALO - 1:HALO, :]
    h_next = hn_ref[0:1, :]
    row = lax.broadcasted_iota(jnp.int32, (tm, 1), 0)
    for c in range(D_FF // FF_CHUNK):
        cs = slice(c * FF_CHUNK, (c + 1) * FF_CHUNK)
        gs = slice(D_FF + c * FF_CHUNK, D_FF + (c + 1) * FF_CHUNK)
        wa = wu_ref[:, cs]
        a = jnp.dot(h2, wa, preferred_element_type=F32)
        a_prev = jnp.where(has_prev, jnp.dot(h_prev, wa, preferred_element_type=F32), 0.0)
        a_next = jnp.where(has_next, jnp.dot(h_next, wa, preferred_element_type=F32), 0.0)
        a_dn = jnp.where(row == 0, a_prev, pltpu.roll(a, 1, axis=0))
        a_up = jnp.where(row == tm - 1, a_next, pltpu.roll(a, tm - 1, axis=0))
        conv = (a_dn * cw_ref[0:1, cs] + a * cw_ref[1:2, cs] + a_up * cw_ref[2:3, cs]
                + cb_ref[:, cs])
        gate = jnp.dot(h2, wu_ref[:, gs], preferred_element_type=F32)
        f_s[:, cs] = (jax.nn.gelu(conv, approximate=True) * gate).astype(BF16)
    y = jnp.dot(f_s[...], wd_ref[...], preferred_element_type=F32)
    out_ref[...] = x1_ref[...] + _rms(y, nw_ref[...])


def _ffn(x1, h2, S, wu, cw, cb, wd, nw):
    T = x1.shape[0]
    tm = min(ROW_TILE, S)
    assert S % tm == 0 and tm % HALO == 0 and D_FF % FF_CHUNK == 0
    nt = T // tm
    hb = tm // HALO
    row = lambda dt: pl.BlockSpec((tm, D_MODEL), lambda i: (i, 0))
    return pl.pallas_call(
        functools.partial(_ffn_kernel, tiles_per_seq=S // tm),
        grid=(nt,),
        in_specs=[row(F32), row(BF16),
                  pl.BlockSpec((HALO, D_MODEL), lambda i: (jnp.maximum(i * hb - 1, 0), 0)),
                  pl.BlockSpec((HALO, D_MODEL),
                               lambda i: (jnp.minimum((i + 1) * hb, nt * hb - 1), 0)),
                  _const_spec((D_MODEL, 2 * D_FF)), _const_spec((3, D_FF)),
                  _const_spec((1, D_FF)), _const_spec((D_FF, D_MODEL)),
                  _const_spec((1, D_MODEL))],
        out_specs=row(F32),
        out_shape=jax.ShapeDtypeStruct((T, D_MODEL), F32),
        scratch_shapes=[pltpu.VMEM((tm, D_FF), BF16)],
        compiler_params=pltpu.CompilerParams(dimension_semantics=("parallel",),
                                             vmem_limit_bytes=VMEM_LIMIT_BYTES),
        name="ffn",
    )(x1, h2, h2, h2, wu, cw, cb, wd, nw)


def _pack_in_projection(w_in, b_in):
    sizes = (1024, 256, 256, 1024, 1024, 1024, 1024, 8, 8, 2048)
    pts = [0]
    for s in sizes:
        pts.append(pts[-1] + s)
    cols = lambda t, k: t[..., pts[k]:pts[k + 1]]
    pad = GATE_SLAB - 16

    def pack(t):
        gate = jnp.concatenate([cols(t, 7), cols(t, 8),
                                jnp.zeros(t.shape[:-1] + (pad,), t.dtype)], axis=-1)
        return jnp.concatenate([cols(t, k) for k in range(7)] + [gate, cols(t, 9)], axis=-1)

    return pack(w_in).astype(BF16), pack(b_in)[None, :]


def _encoder_layer(x, p):
    B, S, _ = x.shape
    T = B * S
    x2d = x.reshape(T, D_MODEL)
    aq, ak, av, mq, mk, mv, mo, gl, ga, gm = _inproj(x2d, p["pre_mix_norm"], p["w_in"], p["b_in"])
    ya = _attention(aq, ak, av, p["attn_sink"], B, S).reshape(T, D_MODEL)
    seq = lambda t: t.reshape(B, S, t.shape[-1])
    q3, k3, v3, g3 = seq(mq), seq(mk), seq(mv), seq(gl)
    hf = _mlstm_scan(q3, k3, v3, g3, reverse=False)
    ym = _mlstm_scan(q3, k3, v3, g3, reverse=True,
                     extra=(hf, seq(mo), p["mlstm_norm"])).reshape(T, D_MODEL)
    x1, h2 = _merge(x2d, ya, ym, ga, gm, p["w_att_branch"], p["w_mlstm_branch"], p["w_out"],
                    p["post_mix_norm"], p["pre_ffn_norm"])
    y = _ffn(x1, h2, S, p["w_ffn_up"], p["ffn_conv_w"], p["ffn_conv_b"], p["w_ffn_down"],
             p["post_ffn_norm"])
    return y.reshape(B, S, D_MODEL)


def kernel(x_prompt, x_sample, pre_mix_norm, w_in, b_in, attn_sink, mlstm_norm, w_att_branch,
           w_mlstm_branch, w_out, post_mix_norm, pre_ffn_norm, w_ffn_up, ffn_conv_w, ffn_conv_b,
           w_ffn_down, post_ffn_norm):
    depth = w_in.shape[0]
    layers = []
    for l in range(depth):
        w_packed, b_packed = _pack_in_projection(w_in[l], b_in[l])
        layers.append(dict(
            pre_mix_norm=pre_mix_norm[l][None, :], w_in=w_packed, b_in=b_packed,
            attn_sink=attn_sink[l], mlstm_norm=mlstm_norm[l][None, :],
            w_att_branch=w_att_branch[l].astype(BF16),
            w_mlstm_branch=w_mlstm_branch[l].astype(BF16), w_out=w_out[l].astype(BF16),
            post_mix_norm=post_mix_norm[l][None, :], pre_ffn_norm=pre_ffn_norm[l][None, :],
            w_ffn_up=w_ffn_up[l].astype(BF16), ffn_conv_w=ffn_conv_w[l],
            ffn_conv_b=ffn_conv_b[l][None, :], w_ffn_down=w_ffn_down[l].astype(BF16),
            post_ffn_norm=post_ffn_norm[l][None, :]))

    def trunk(x):
        for p in layers:
            x = _encoder_layer(x, p)
        return x

    return (trunk(x_prompt), trunk(x_sample))
```

```python
import functools

import jax
import jax.numpy as jnp
from jax import lax
from jax.experimental import pallas as pl
from jax.experimental.pallas import tpu as pltpu

F32 = jnp.float32
BF16 = jnp.bfloat16

D_MODEL = 1024
N_ATT_HEADS = 16
N_KV_GROUPS = 4
ATT_HEAD_DIM = 64
HEADS_PER_GROUP = N_ATT_HEADS // N_KV_GROUPS
KV_WIDTH = N_KV_GROUPS * ATT_HEAD_DIM
WINDOW = 128
BLOCK = 128
N_MLSTM_HEADS = 4
MLSTM_HEAD_DIM = 256
FORGET_BIAS = 3.0
D_FF = 2816
EPS = 1e-6
MASK_VALUE = -1e30
LOG2E = 1.4426950408889634

LANES = 128
GATE_SLAB = LANES
V_SLAB = LANES
VMEM_LIMIT_BYTES = 56 * 1024 * 1024

_SLABS = {}
_off = 0
for _name, _width in (("aq", 1024), ("ak", 256), ("av", N_KV_GROUPS * V_SLAB),
                      ("mq", 1024), ("mk", 1024),
                      ("mv", 1024), ("mo", 1024), ("gl", GATE_SLAB), ("ga", 1024), ("gm", 1024)):
    _SLABS[_name] = (_off, _width)
    _off += _width
PACKED_IN_WIDTH = _off

ROW_TILE = 512
FF_CHUNK = 256


def _const_spec(shape):
    nd = len(shape)
    return pl.BlockSpec(shape, lambda *_: (0,) * nd, pipeline_mode=pl.Buffered(1))


def _rms(x, w):
    return x * lax.rsqrt(jnp.mean(x * x, axis=-1, keepdims=True) + EPS) * w


def _inproj_kernel(x_ref, nw_ref, w_ref, b_ref, aq_ref, ak_ref, av_ref, mq_ref, mk_ref,
                   mv_ref, mo_ref, gl_ref, ga_ref, gm_ref):
    h = _rms(x_ref[...], nw_ref[...]).astype(BF16)

    def proj(name):
        s, w = _SLABS[name]
        return jnp.dot(h, w_ref[:, s:s + w], preferred_element_type=F32) + b_ref[:, s:s + w]

    aq_ref[...] = (proj("aq") * (ATT_HEAD_DIM ** -0.5 * LOG2E)).astype(BF16)
    ak_ref[...] = proj("ak").astype(BF16)
    av_ref[...] = proj("av").astype(BF16)
    mq_ref[...] = proj("mq").astype(BF16)
    mk_ref[...] = (proj("mk") * (MLSTM_HEAD_DIM ** -0.5)).astype(BF16)
    mv_ref[...] = proj("mv").astype(BF16)
    mo_ref[...] = proj("mo").astype(BF16)
    gl_ref[...] = proj("gl")
    ga_ref[...] = jax.nn.sigmoid(proj("ga")).astype(BF16)
    gm_ref[...] = jax.nn.sigmoid(proj("gm")).astype(BF16)


def _inproj(x2d, norm_w, w_packed, b_packed):
    T = x2d.shape[0]
    tm = min(ROW_TILE, T)
    assert T % tm == 0
    row = lambda w: pl.BlockSpec((tm, w), lambda i: (i, 0))
    names = ("aq", "ak", "av", "mq", "mk", "mv", "mo", "gl", "ga", "gm")
    out_shape = tuple(jax.ShapeDtypeStruct((T, _SLABS[n][1]), F32 if n == "gl" else BF16)
                      for n in names)
    return pl.pallas_call(
        _inproj_kernel,
        grid=(T // tm,),
        in_specs=[row(D_MODEL), _const_spec((1, D_MODEL)),
                  _const_spec((D_MODEL, PACKED_IN_WIDTH)), _const_spec((1, PACKED_IN_WIDTH))],
        out_specs=tuple(row(_SLABS[n][1]) for n in names),
        out_shape=out_shape,
        compiler_params=pltpu.CompilerParams(dimension_semantics=("parallel",),
                                             vmem_limit_bytes=VMEM_LIMIT_BYTES),
        name="inproj",
    )(x2d, norm_w, w_packed, b_packed)


def _attn_kernel(sink_ref, q_ref, k_ref, v_ref, bias_ref, o_ref):
    hd = ATT_HEAD_DIM
    j, nb = pl.program_id(1), pl.num_programs(1)
    variant = jnp.where(j == 0, 0, jnp.where(j == nb - 1, 2, 1))
    rows = [pl.ds(pl.multiple_of(blk * BLOCK, BLOCK), BLOCK)
            for blk in (jnp.maximum(j - 1, 0), j, jnp.minimum(j + 1, nb - 1))]
    for g in range(N_KV_GROUPS):
        kg = jnp.concatenate([k_ref[0, r, g * hd:(g + 1) * hd] for r in rows], axis=0)
        vg = jnp.concatenate([v_ref[0, r, g * V_SLAB:(g + 1) * V_SLAB] for r in rows], axis=0)
        heads = [g * HEADS_PER_GROUP + r for r in range(HEADS_PER_GROUP)]
        qs = jnp.concatenate([q_ref[0, :, h * hd:(h + 1) * hd] for h in heads], axis=0)
        s = lax.dot_general(qs, kg, (((1,), (1,)), ((), ())), preferred_element_type=F32)
        ps, sink_terms = [], []
        for r, h in enumerate(heads):
            sh = s[r * BLOCK:(r + 1) * BLOCK] + bias_ref[variant, h]
            sink = sink_ref[h] * LOG2E
            m = jnp.maximum(jnp.max(sh, axis=-1, keepdims=True), sink)
            ps.append(jnp.exp2(sh - m).astype(BF16))
            sink_terms.append(jnp.exp2(sink - m))
        o = jnp.dot(jnp.concatenate(ps, axis=0), vg, preferred_element_type=F32)
        outs = []
        for r in range(HEADS_PER_GROUP):
            o_r = o[r * BLOCK:(r + 1) * BLOCK]
            outs.append(o_r[:, :hd] / (o_r[:, hd:hd + 1] + sink_terms[r]))
        gs = slice(g * HEADS_PER_GROUP * hd, (g + 1) * HEADS_PER_GROUP * hd)
        o_ref[0, :, gs] = jnp.concatenate(outs, axis=1).astype(BF16)


def _attention_bias():
    rel = (jnp.arange(3 * BLOCK)[None, :] - BLOCK) - jnp.arange(BLOCK)[:, None]
    slopes = 2.0 ** (-8.0 * jnp.arange(1, N_ATT_HEADS + 1, dtype=F32) / N_ATT_HEADS)
    alibi = -slopes[:, None, None] * jnp.abs(rel).astype(F32)[None] * LOG2E
    band = (jnp.abs(rel) <= WINDOW)[None]
    col = jnp.arange(3 * BLOCK)[None, None, :]
    variants = []
    for has_prev, has_next in ((False, True), (True, True), (True, False)):
        ok = band & ((col >= BLOCK) | has_prev) & ((col < 2 * BLOCK) | has_next)
        variants.append(jnp.where(ok, alibi, MASK_VALUE))
    return jnp.stack(variants)


def _attention(aq, ak, av, sink, B, S):
    nb = S // BLOCK
    assert S % BLOCK == 0 and nb >= 2
    q3 = aq.reshape(B, S, N_ATT_HEADS * ATT_HEAD_DIM)
    k3 = ak.reshape(B, S, KV_WIDTH)
    v3 = av.reshape(B, S, N_KV_GROUPS * V_SLAB)
    whole_seq = lambda w: pl.BlockSpec((1, S, w), lambda b, j: (b, 0, 0))
    return pl.pallas_call(
        _attn_kernel,
        grid=(B, nb),
        in_specs=[pl.BlockSpec(memory_space=pltpu.SMEM),
                  pl.BlockSpec((1, BLOCK, D_MODEL), lambda b, j: (b, j, 0)),
                  whole_seq(KV_WIDTH), whole_seq(N_KV_GROUPS * V_SLAB),
                  _const_spec((3, N_ATT_HEADS, BLOCK, 3 * BLOCK))],
        out_specs=pl.BlockSpec((1, BLOCK, D_MODEL), lambda b, j: (b, j, 0)),
        out_shape=jax.ShapeDtypeStruct((B, S, D_MODEL), BF16),
        compiler_params=pltpu.CompilerParams(dimension_semantics=("arbitrary", "arbitrary"),
                                             vmem_limit_bytes=VMEM_LIMIT_BYTES),
        name="attention",
    )(sink, q3, k3, v3, _attention_bias())


MLSTM_CHUNK = 256
AUG = MLSTM_HEAD_DIM + LANES


def _mlstm_kernel(q_ref, k_ref, v_ref, g_ref, mo_ref, nw_ref, out_ref, c_s, m_s, hb_s):
    L, dh, nh = MLSTM_CHUNK, MLSTM_HEAD_DIM, N_MLSTM_HEADS
    phase, c = pl.program_id(1), pl.program_id(2)
    nc = pl.num_programs(2)
    is_bwd = phase == 0
    row0 = pl.multiple_of(jnp.where(is_bwd, nc - 1 - c, c) * L, L)

    @pl.when(c == 0)
    def _():
        c_s[...] = jnp.zeros_like(c_s)
        m_s[...] = jnp.zeros_like(m_s)

    gl = g_ref[0]
    lf_all = jax.nn.log_sigmoid(gl + FORGET_BIAS)
    lag = (lax.broadcasted_iota(jnp.int32, (L, L), 0)
           - lax.broadcasted_iota(jnp.int32, (L, L), 1))
    seen = jnp.where(is_bwd, -lag, lag) >= 0
    unseen_bias = jnp.where(seen, 0.0, -jnp.inf)
    seen_bf = seen.astype(BF16)
    lf_hi = lf_all.astype(BF16)
    lf_r = lf_all - lf_hi.astype(F32)
    lf_mid = lf_r.astype(BF16)
    lf_lo = (lf_r - lf_mid.astype(F32)).astype(BF16)
    bcol_all = (jnp.dot(seen_bf, lf_hi, preferred_element_type=F32)
                + jnp.dot(seen_bf, lf_mid, preferred_element_type=F32)
                + jnp.dot(seen_bf, lf_lo, preferred_element_type=F32))
    brow_all = bcol_all.T
    lirow_all = gl.T
    ones_col = (lax.broadcasted_iota(jnp.int32, (L, LANES), 1) == 0).astype(BF16)

    def pick(bwd_val, fwd_val):
        return jnp.where(is_bwd, bwd_val, fwd_val)

    for h in range(nh):
        ci_f, ci_b = h, nh + h
        cf_f, cf_b = 2 * nh + h, 3 * nh + h
        hs = slice(h * dh, (h + 1) * dh)
        b_col = pick(bcol_all[:, cf_b:cf_b + 1], bcol_all[:, cf_f:cf_f + 1])
        b_row = pick(brow_all[cf_b:cf_b + 1, :], brow_all[cf_f:cf_f + 1, :])
        li_col = pick(gl[:, ci_b:ci_b + 1], gl[:, ci_f:ci_f + 1])
        li_row = pick(lirow_all[ci_b:ci_b + 1, :], lirow_all[ci_f:ci_f + 1, :])
        b_last = pick(b_col[0:1, :], b_col[L - 1:L, :])
        m_prev = m_s[h][:, 0:1]
        c_prev = c_s[h]
        qh, kh = q_ref[0, :, hs], k_ref[0, :, hs]
        v_aug = jnp.concatenate([v_ref[0, :, hs], ones_col], axis=1)

        d = (b_col + (li_row - b_row)) + unseen_bias
        m_inter = b_col + m_prev
        m_t = jnp.maximum(m_inter, jnp.max(d, axis=-1, keepdims=True))
        w_inter = jnp.exp(m_inter - m_t)
        qk = lax.dot_general(qh, kh, (((1,), (1,)), ((), ())), preferred_element_type=F32)
        a = (qk * jnp.exp(d - m_t)).astype(BF16)
        numden = (w_inter * jnp.dot(qh, c_prev.astype(BF16), preferred_element_type=F32)
                  + jnp.dot(a, v_aug, preferred_element_type=F32))
        den = numden[:, dh:dh + 1]
        hh = numden[:, :dh] / jnp.maximum(jnp.abs(den), jnp.exp(-m_t))

        g_col = b_last - b_col + li_col
        m_new = jnp.maximum(b_last + m_prev, jnp.max(g_col, axis=0, keepdims=True))
        decay = jnp.exp(b_last + m_prev - m_new)
        wk = (jnp.exp(g_col - m_new) * kh.astype(F32)).astype(BF16)
        c_s[h] = decay * c_prev + lax.dot_general(
            wk, v_aug, (((0,), (0,)), ((), ())), preferred_element_type=F32)
        m_s[h] = jnp.broadcast_to(m_new, (1, LANES))

        @pl.when(is_bwd)
        def _():
            hb_s[pl.ds(row0, L), hs] = hh.astype(hb_s.dtype)

        @pl.when(jnp.logical_not(is_bwd))
        def _():
            hsum = hh + hb_s[pl.ds(row0, L), hs].astype(F32)
            o = jax.nn.sigmoid(mo_ref[0, :, hs].astype(F32)) * hsum
            out_ref[0, :, hs] = _rms(o, nw_ref[:, hs]).astype(out_ref.dtype)


def _mlstm(q3, k3, v3, g3, mo3, norm_w):
    B, S, W = q3.shape
    L = MLSTM_CHUNK
    nc = S // L
    assert S % L == 0
    both = lambda p, c: jnp.where(p == 0, nc - 1 - c, c)
    fwd_only = lambda p, c: jnp.where(p == 0, 0, c)
    seq = lambda w, f: pl.BlockSpec((1, L, w), lambda b, p, c: (b, f(p, c), 0))
    return pl.pallas_call(
        _mlstm_kernel,
        grid=(B, 2, nc),
        in_specs=[seq(W, both), seq(W, both), seq(W, both), seq(GATE_SLAB, both),
                  seq(W, fwd_only), _const_spec((1, W))],
        out_specs=seq(W, fwd_only),
        out_shape=jax.ShapeDtypeStruct((B, S, W), BF16),
        scratch_shapes=[pltpu.VMEM((N_MLSTM_HEADS, MLSTM_HEAD_DIM, AUG), F32),
                        pltpu.VMEM((N_MLSTM_HEADS, 1, LANES), F32),
                        pltpu.VMEM((S, W), BF16)],
        compiler_params=pltpu.CompilerParams(
            dimension_semantics=("arbitrary", "arbitrary", "arbitrary"),
            vmem_limit_bytes=VMEM_LIMIT_BYTES),
        name="mlstm",
    )(q3, k3, v3, g3, mo3, norm_w)


def _merge_kernel(x_ref, ya_ref, ym_ref, ga_ref, gm_ref, wa_ref, wm_ref, wo_ref, pmw_ref,
                  pfw_ref, x1_ref, h2_ref):
    merged = (ga_ref[...].astype(F32) * jnp.dot(ya_ref[...], wa_ref[...],
                                                 preferred_element_type=F32)
              + gm_ref[...].astype(F32) * jnp.dot(ym_ref[...], wm_ref[...],
                                                   preferred_element_type=F32))
    mixed = jnp.dot(merged.astype(BF16), wo_ref[...], preferred_element_type=F32)
    x1 = x_ref[...] + _rms(mixed, pmw_ref[...])
    x1_ref[...] = x1
    h2_ref[...] = _rms(x1, pfw_ref[...]).astype(BF16)


def _merge(x2d, ya, ym, ga, gm, wa, wm, wo, pmw, pfw):
    T = x2d.shape[0]
    tm = min(ROW_TILE, T)
    assert T % tm == 0
    row = pl.BlockSpec((tm, D_MODEL), lambda i: (i, 0))
    sq = _const_spec((D_MODEL, D_MODEL))
    vec = _const_spec((1, D_MODEL))
    return pl.pallas_call(
        _merge_kernel,
        grid=(T // tm,),
        in_specs=[row, row, row, row, row, sq, sq, sq, vec, vec],
        out_specs=(row, row),
        out_shape=(jax.ShapeDtypeStruct((T, D_MODEL), F32),
                   jax.ShapeDtypeStruct((T, D_MODEL), BF16)),
        compiler_params=pltpu.CompilerParams(dimension_semantics=("parallel",),
                                             vmem_limit_bytes=VMEM_LIMIT_BYTES),
        name="merge",
    )(x2d, ya, ym, ga, gm, wa, wm, wo, pmw, pfw)


HALO = 8


def _ffn_kernel(x1_ref, h2_ref, hp_ref, hn_ref, wu_ref, cw_ref, cb_ref, wd_ref, nw_ref,
                out_ref, f_s, *, tiles_per_seq):
    i = pl.program_id(0)
    tm = h2_ref.shape[0]
    has_prev = (i % tiles_per_seq) != 0
    has_next = (i % tiles_per_seq) != tiles_per_seq - 1
    h2 = h2_ref[...]
    h_prev = hp_ref[HALO - 1:HALO, :]
    h_next = hn_ref[0:1, :]
    row = lax.broadcasted_iota(jnp.int32, (tm, 1), 0)
    for c in range(D_FF // FF_CHUNK):
        cs = slice(c * FF_CHUNK, (c + 1) * FF_CHUNK)
        gs = slice(D_FF + c * FF_CHUNK, D_FF + (c + 1) * FF_CHUNK)
        wa = wu_ref[:, cs]
        a = jnp.dot(h2, wa, preferred_element_type=F32)
        a_prev = jnp.where(has_prev, jnp.dot(h_prev, wa, preferred_element_type=F32), 0.0)
        a_next = jnp.where(has_next, jnp.dot(h_next, wa, preferred_element_type=F32), 0.0)
        a_dn = jnp.where(row == 0, a_prev, pltpu.roll(a, 1, axis=0))
        a_up = jnp.where(row == tm - 1, a_next, pltpu.roll(a, tm - 1, axis=0))
        conv = (a_dn * cw_ref[0:1, cs] + a * cw_ref[1:2, cs] + a_up * cw_ref[2:3, cs]
                + cb_ref[:, cs])
        gate = jnp.dot(h2, wu_ref[:, gs], preferred_element_type=F32)
        f_s[:, cs] = (jax.nn.gelu(conv, approximate=True) * gate).astype(BF16)
    y = jnp.dot(f_s[...], wd_ref[...], preferred_element_type=F32)
    out_ref[...] = x1_ref[...] + _rms(y, nw_ref[...])


def _ffn(x1, h2, S, wu, cw, cb, wd, nw):
    T = x1.shape[0]
    tm = min(ROW_TILE, S)
    assert S % tm == 0 and tm % HALO == 0 and D_FF % FF_CHUNK == 0
    nt = T // tm
    hb = tm // HALO
    row = lambda dt: pl.BlockSpec((tm, D_MODEL), lambda i: (i, 0))
    return pl.pallas_call(
        functools.partial(_ffn_kernel, tiles_per_seq=S // tm),
        grid=(nt,),
        in_specs=[row(F32), row(BF16),
                  pl.BlockSpec((HALO, D_MODEL), lambda i: (jnp.maximum(i * hb - 1, 0), 0)),
                  pl.BlockSpec((HALO, D_MODEL),
                               lambda i: (jnp.minimum((i + 1) * hb, nt * hb - 1), 0)),
                  _const_spec((D_MODEL, 2 * D_FF)), _const_spec((3, D_FF)),
                  _const_spec((1, D_FF)), _const_spec((D_FF, D_MODEL)),
                  _const_spec((1, D_MODEL))],
        out_specs=row(F32),
        out_shape=jax.ShapeDtypeStruct((T, D_MODEL), F32),
        scratch_shapes=[pltpu.VMEM((tm, D_FF), BF16)],
        compiler_params=pltpu.CompilerParams(dimension_semantics=("parallel",),
                                             vmem_limit_bytes=VMEM_LIMIT_BYTES),
        name="ffn",
    )(x1, h2, h2, h2, wu, cw, cb, wd, nw)


def _pack_in_projection(w_in, b_in):
    sizes = (1024, 256, 256, 1024, 1024, 1024, 1024, 8, 8, 2048)
    pts = [0]
    for s in sizes:
        pts.append(pts[-1] + s)
    cols = lambda t, k: t[..., pts[k]:pts[k + 1]]
    hd = ATT_HEAD_DIM

    def pack(t, one):
        zeros = lambda n: jnp.zeros(t.shape[:-1] + (n,), t.dtype)
        gate = jnp.concatenate([cols(t, 7), cols(t, 8), zeros(GATE_SLAB - 16)], axis=-1)
        v = cols(t, 2)
        ones_lane = jnp.full(t.shape[:-1] + (1,), one, t.dtype)
        v_slabs = [jnp.concatenate([v[..., g * hd:(g + 1) * hd], ones_lane,
                                    zeros(V_SLAB - hd - 1)], axis=-1)
                   for g in range(N_KV_GROUPS)]
        return jnp.concatenate([cols(t, 0), cols(t, 1)] + v_slabs
                               + [cols(t, k) for k in range(3, 7)] + [gate, cols(t, 9)], axis=-1)

    return pack(w_in, 0.0).astype(BF16), pack(b_in, 1.0)[None, :]


def _encoder_layer(x, p):
    B, S, _ = x.shape
    T = B * S
    x2d = x.reshape(T, D_MODEL)
    aq, ak, av, mq, mk, mv, mo, gl, ga, gm = _inproj(x2d, p["pre_mix_norm"], p["w_in"], p["b_in"])
    ya = _attention(aq, ak, av, p["attn_sink"], B, S).reshape(T, D_MODEL)
    seq = lambda t: t.reshape(B, S, t.shape[-1])
    q3, k3, v3, g3 = seq(mq), seq(mk), seq(mv), seq(gl)
    ym = _mlstm(q3, k3, v3, g3, seq(mo), p["mlstm_norm"]).reshape(T, D_MODEL)
    x1, h2 = _merge(x2d, ya, ym, ga, gm, p["w_att_branch"], p["w_mlstm_branch"], p["w_out"],
                    p["post_mix_norm"], p["pre_ffn_norm"])
    y = _ffn(x1, h2, S, p["w_ffn_up"], p["ffn_conv_w"], p["ffn_conv_b"], p["w_ffn_down"],
             p["post_ffn_norm"])
    return y.reshape(B, S, D_MODEL)


def kernel(x_prompt, x_sample, pre_mix_norm, w_in, b_in, attn_sink, mlstm_norm, w_att_branch,
           w_mlstm_branch, w_out, post_mix_norm, pre_ffn_norm, w_ffn_up, ffn_conv_w, ffn_conv_b,
           w_ffn_down, post_ffn_norm):
    depth = w_in.shape[0]
    layers = []
    for l in range(depth):
        w_packed, b_packed = _pack_in_projection(w_in[l], b_in[l])
        layers.append(dict(
            pre_mix_norm=pre_mix_norm[l][None, :], w_in=w_packed, b_in=b_packed,
            attn_sink=attn_sink[l], mlstm_norm=mlstm_norm[l][None, :],
            w_att_branch=w_att_branch[l].astype(BF16),
            w_mlstm_branch=w_mlstm_branch[l].astype(BF16), w_out=w_out[l].astype(BF16),
            post_mix_norm=post_mix_norm[l][None, :], pre_ffn_norm=pre_ffn_norm[l][None, :],
            w_ffn_up=w_ffn_up[l].astype(BF16), ffn_conv_w=ffn_conv_w[l],
            ffn_conv_b=ffn_conv_b[l][None, :], w_ffn_down=w_ffn_down[l].astype(BF16),
            post_ffn_norm=post_ffn_norm[l][None, :]))

    def trunk(x):
        for p in layers:
            x = _encoder_layer(x, p)
        return x

    return (trunk(x_prompt), trunk(x_sample))
```

```python
import functools

import jax
import jax.numpy as jnp
from jax import lax
from jax.experimental import pallas as pl
from jax.experimental.pallas import tpu as pltpu

F32 = jnp.float32
BF16 = jnp.bfloat16

D_MODEL = 1024
N_ATT_HEADS = 16
N_KV_GROUPS = 4
ATT_HEAD_DIM = 64
HEADS_PER_GROUP = N_ATT_HEADS // N_KV_GROUPS
KV_WIDTH = N_KV_GROUPS * ATT_HEAD_DIM
WINDOW = 128
BLOCK = 128
N_MLSTM_HEADS = 4
MLSTM_HEAD_DIM = 256
FORGET_BIAS = 3.0
D_FF = 2816
EPS = 1e-6
MASK_VALUE = -1e30
LOG2E = 1.4426950408889634

LANES = 128
GATE_SLAB = LANES
V_SLAB = LANES
VMEM_LIMIT_BYTES = 56 * 1024 * 1024

_SLABS = {}
_off = 0
for _name, _width in (("aq", 1024), ("ak", 256), ("av", N_KV_GROUPS * V_SLAB),
                      ("mq", 1024), ("mk", 1024),
                      ("mv", 1024), ("mo", 1024), ("gl", GATE_SLAB), ("ga", 1024), ("gm", 1024)):
    _SLABS[_name] = (_off, _width)
    _off += _width
PACKED_IN_WIDTH = _off

ROW_TILE = 512
MERGE_SUBTILES = 2
FF_CHUNK = 256


def _const_spec(shape):
    nd = len(shape)
    return pl.BlockSpec(shape, lambda *_: (0,) * nd, pipeline_mode=pl.Buffered(1))


def _rms(x, w):
    return x * lax.rsqrt(jnp.mean(x * x, axis=-1, keepdims=True) + EPS) * w


def _emit_staggered(stages, n):
    for step in range(n + len(stages) - 1):
        for depth, stage in enumerate(stages):
            if 0 <= step - depth < n:
                stage(step - depth)


def _inproj_kernel(x_ref, nw_ref, w_ref, b_ref, aq_ref, ak_ref, av_ref, mq_ref, mk_ref,
                   mv_ref, mo_ref, gl_ref, ga_ref, gm_ref):
    h = _rms(x_ref[...], nw_ref[...]).astype(BF16)

    def proj(name):
        s, w = _SLABS[name]
        return jnp.dot(h, w_ref[:, s:s + w], preferred_element_type=F32) + b_ref[:, s:s + w]

    aq_ref[...] = (proj("aq") * (ATT_HEAD_DIM ** -0.5 * LOG2E)).astype(BF16)
    ak_ref[...] = proj("ak").astype(BF16)
    av_ref[...] = proj("av").astype(BF16)
    mq_ref[...] = proj("mq").astype(BF16)
    mk_ref[...] = (proj("mk") * (MLSTM_HEAD_DIM ** -0.5)).astype(BF16)
    mv_ref[...] = proj("mv").astype(BF16)
    mo_ref[...] = proj("mo").astype(BF16)
    gl_ref[...] = proj("gl")
    ga_ref[...] = jax.nn.sigmoid(proj("ga")).astype(BF16)
    gm_ref[...] = jax.nn.sigmoid(proj("gm")).astype(BF16)


def _inproj(x2d, norm_w, w_packed, b_packed):
    T = x2d.shape[0]
    tm = min(ROW_TILE, T)
    assert T % tm == 0
    row = lambda w: pl.BlockSpec((tm, w), lambda i: (i, 0))
    names = ("aq", "ak", "av", "mq", "mk", "mv", "mo", "gl", "ga", "gm")
    out_shape = tuple(jax.ShapeDtypeStruct((T, _SLABS[n][1]), F32 if n == "gl" else BF16)
                      for n in names)
    return pl.pallas_call(
        _inproj_kernel,
        grid=(T // tm,),
        in_specs=[row(D_MODEL), _const_spec((1, D_MODEL)),
                  _const_spec((D_MODEL, PACKED_IN_WIDTH)), _const_spec((1, PACKED_IN_WIDTH))],
        out_specs=tuple(row(_SLABS[n][1]) for n in names),
        out_shape=out_shape,
        compiler_params=pltpu.CompilerParams(dimension_semantics=("parallel",),
                                             vmem_limit_bytes=VMEM_LIMIT_BYTES),
        name="inproj",
    )(x2d, norm_w, w_packed, b_packed)


def _attn_kernel(sink_ref, q_ref, k_ref, v_ref, bias_ref, o_ref):
    hd = ATT_HEAD_DIM
    j, nb = pl.program_id(1), pl.num_programs(1)
    variant = jnp.where(j == 0, 0, jnp.where(j == nb - 1, 2, 1))
    rows = [pl.ds(pl.multiple_of(blk * BLOCK, BLOCK), BLOCK)
            for blk in (jnp.maximum(j - 1, 0), j, jnp.minimum(j + 1, nb - 1))]
    groups = range(N_KV_GROUPS)
    scores = []
    for g in groups:
        kg = jnp.concatenate([k_ref[0, r, g * hd:(g + 1) * hd] for r in rows], axis=0)
        qs = jnp.concatenate([q_ref[0, :, (g * HEADS_PER_GROUP + r) * hd:
                                    (g * HEADS_PER_GROUP + r + 1) * hd]
                              for r in range(HEADS_PER_GROUP)], axis=0)
        scores.append(lax.dot_general(qs, kg, (((1,), (1,)), ((), ())),
                                      preferred_element_type=F32))
    for g in groups:
        vg = jnp.concatenate([v_ref[0, r, g * V_SLAB:(g + 1) * V_SLAB] for r in rows], axis=0)
        ps, sink_terms = [], []
        for r in range(HEADS_PER_GROUP):
            h = g * HEADS_PER_GROUP + r
            sh = scores[g][r * BLOCK:(r + 1) * BLOCK] + bias_ref[variant, h]
            sink = sink_ref[h] * LOG2E
            m = jnp.maximum(jnp.max(sh, axis=-1, keepdims=True), sink)
            ps.append(jnp.exp2(sh - m).astype(BF16))
            sink_terms.append(jnp.exp2(sink - m))
        o = jnp.dot(jnp.concatenate(ps, axis=0), vg, preferred_element_type=F32)
        outs = []
        for r in range(HEADS_PER_GROUP):
            o_r = o[r * BLOCK:(r + 1) * BLOCK]
            outs.append(o_r[:, :hd] / (o_r[:, hd:hd + 1] + sink_terms[r]))
        gs = slice(g * HEADS_PER_GROUP * hd, (g + 1) * HEADS_PER_GROUP * hd)
        o_ref[0, :, gs] = jnp.concatenate(outs, axis=1).astype(BF16)


def _attention_bias():
    rel = (jnp.arange(3 * BLOCK)[None, :] - BLOCK) - jnp.arange(BLOCK)[:, None]
    slopes = 2.0 ** (-8.0 * jnp.arange(1, N_ATT_HEADS + 1, dtype=F32) / N_ATT_HEADS)
    alibi = -slopes[:, None, None] * jnp.abs(rel).astype(F32)[None] * LOG2E
    band = (jnp.abs(rel) <= WINDOW)[None]
    col = jnp.arange(3 * BLOCK)[None, None, :]
    variants = []
    for has_prev, has_next in ((False, True), (True, True), (True, False)):
        ok = band & ((col >= BLOCK) | has_prev) & ((col < 2 * BLOCK) | has_next)
        variants.append(jnp.where(ok, alibi, MASK_VALUE))
    return jnp.stack(variants)


def _attention(aq, ak, av, sink, B, S):
    nb = S // BLOCK
    assert S % BLOCK == 0 and nb >= 2
    q3 = aq.reshape(B, S, N_ATT_HEADS * ATT_HEAD_DIM)
    k3 = ak.reshape(B, S, KV_WIDTH)
    v3 = av.reshape(B, S, N_KV_GROUPS * V_SLAB)
    whole_seq = lambda w: pl.BlockSpec((1, S, w), lambda b, j: (b, 0, 0))
    return pl.pallas_call(
        _attn_kernel,
        grid=(B, nb),
        in_specs=[pl.BlockSpec(memory_space=pltpu.SMEM),
                  pl.BlockSpec((1, BLOCK, D_MODEL), lambda b, j: (b, j, 0)),
                  whole_seq(KV_WIDTH), whole_seq(N_KV_GROUPS * V_SLAB),
                  _const_spec((3, N_ATT_HEADS, BLOCK, 3 * BLOCK))],
        out_specs=pl.BlockSpec((1, BLOCK, D_MODEL), lambda b, j: (b, j, 0)),
        out_shape=jax.ShapeDtypeStruct((B, S, D_MODEL), BF16),
        compiler_params=pltpu.CompilerParams(dimension_semantics=("arbitrary", "arbitrary"),
                                             vmem_limit_bytes=VMEM_LIMIT_BYTES),
        name="attention",
    )(sink, q3, k3, v3, _attention_bias())


MLSTM_CHUNK = 256
AUG = MLSTM_HEAD_DIM + LANES


def _lanes(x, tiles):
    return jnp.concatenate([x] * tiles, axis=1)


def _gate_vectors(g_ref, reverse):
    L, nh = MLSTM_CHUNK, N_MLSTM_HEADS
    gl_raw = g_ref[0]
    gl = gl_raw * LOG2E
    lf_all = jax.nn.log_sigmoid(gl_raw + FORGET_BIAS) * LOG2E
    t_idx = lax.broadcasted_iota(jnp.int32, (L, L), 0)
    s_idx = lax.broadcasted_iota(jnp.int32, (L, L), 1)
    seen = (s_idx >= t_idx) if reverse else (s_idx <= t_idx)
    unseen_bias = jnp.where(seen, 0.0, -jnp.inf)
    seen_bf = seen.astype(BF16)
    lf_hi = lf_all.astype(BF16)
    lf_r = lf_all - lf_hi.astype(F32)
    lf_mid = lf_r.astype(BF16)
    lf_lo = (lf_r - lf_mid.astype(F32)).astype(BF16)
    bcol_all = (jnp.dot(seen_bf, lf_hi, preferred_element_type=F32)
                + jnp.dot(seen_bf, lf_mid, preferred_element_type=F32)
                + jnp.dot(seen_bf, lf_lo, preferred_element_type=F32))
    brow_all = bcol_all.T
    lirow_all = gl.T
    direction = 1 if reverse else 0
    last = 0 if reverse else L - 1
    per_head = []
    for h in range(nh):
        ci = direction * nh + h
        cf = 2 * nh + ci
        b_rep = jnp.broadcast_to(bcol_all[:, cf:cf + 1], (L, LANES))
        per_head.append(dict(b=b_rep,
                             li=jnp.broadcast_to(gl[:, ci:ci + 1], (L, LANES)),
                             u_row=lirow_all[ci:ci + 1, :] - brow_all[cf:cf + 1, :],
                             b_last=b_rep[last:last + 1, :]))
    return unseen_bias, per_head


def _mlstm_kernel(qf_ref, kf_ref, vf_ref, gf_ref, qb_ref, kb_ref, vb_ref, gb_ref,
                  hf_ref, hb_ref, c_s, m_s):
    L, dh, nh = MLSTM_CHUNK, MLSTM_HEAD_DIM, N_MLSTM_HEADS
    n_aug = AUG // LANES

    @pl.when(pl.program_id(1) == 0)
    def _():
        c_s[...] = jnp.zeros_like(c_s)
        m_s[...] = jnp.zeros_like(m_s)

    ones_block = jnp.ones((L, LANES), BF16)
    nt = (((1,), (1,)), ((), ()))
    tn = (((0,), (0,)), ((), ()))

    chains = []
    for direction, (q_ref, k_ref, v_ref, o_ref) in enumerate(
            ((qf_ref, kf_ref, vf_ref, hf_ref), (qb_ref, kb_ref, vb_ref, hb_ref))):
        for h in range(nh):
            chains.append((q_ref, k_ref, v_ref, o_ref, slice(h * dh, (h + 1) * dh),
                           direction * nh + h))
    n = len(chains)
    st = [dict() for _ in range(n)]

    mask_f, gates_f = _gate_vectors(gf_ref, reverse=False)
    mask_b, gates_b = _gate_vectors(gb_ref, reverse=True)
    gates = gates_f + gates_b
    masks = [mask_f] * nh + [mask_b] * nh

    for i, (q_ref, k_ref, v_ref, _, hs, slot) in enumerate(chains):
        s = st[i]
        s["c_prev"] = c_s[slot]
        s["m_prev"] = m_s[slot]
        s["v_aug"] = jnp.concatenate([v_ref[0, :, hs], ones_block], axis=1)
        s["qk"] = lax.dot_general(q_ref[0, :, hs], k_ref[0, :, hs], nt,
                                  preferred_element_type=F32)
        s["inter"] = jnp.dot(q_ref[0, :, hs], s["c_prev"].astype(BF16),
                             preferred_element_type=F32)

    for i, (_, k_ref, _, _, hs, slot) in enumerate(chains):
        s, gv = st[i], gates[i]
        g_rep = gv["b_last"] - gv["b"] + gv["li"]
        m_new = jnp.maximum(gv["b_last"] + s["m_prev"], jnp.max(g_rep, axis=0, keepdims=True))
        decay = jnp.exp2(gv["b_last"] + s["m_prev"] - m_new)
        wk = (_lanes(jnp.exp2(g_rep - m_new), dh // LANES)
              * k_ref[0, :, hs].astype(F32)).astype(BF16)
        c_s[slot] = _lanes(decay, n_aug) * s["c_prev"] + lax.dot_general(
            wk, s["v_aug"], tn, preferred_element_type=F32)
        m_s[slot] = m_new

    for i in range(n):
        s, gv = st[i], gates[i]
        d = (_lanes(gv["b"], L // LANES) + gv["u_row"]) + masks[i]
        s["m_t"] = jnp.maximum(gv["b"] + s["m_prev"], jnp.max(d, axis=-1, keepdims=True))
        a = (s["qk"] * jnp.exp2(d - _lanes(s["m_t"], L // LANES))).astype(BF16)
        s["intra"] = jnp.dot(a, s["v_aug"], preferred_element_type=F32)

    for i, (_, _, _, o_ref, hs, _) in enumerate(chains):
        s, gv = st[i], gates[i]
        w_inter = jnp.exp2(gv["b"] + s["m_prev"] - s["m_t"])
        numden = _lanes(w_inter, n_aug) * s["inter"] + s["intra"]
        den = numden[:, dh:]
        scale = 1.0 / jnp.maximum(jnp.abs(den), jnp.exp2(-s["m_t"]))
        o_ref[0, :, hs] = (numden[:, :dh] * _lanes(scale, dh // LANES)).astype(o_ref.dtype)


def _mlstm(q3, k3, v3, g3):
    B, S, W = q3.shape
    L = MLSTM_CHUNK
    nc = S // L
    assert S % L == 0
    fwd = lambda w: pl.BlockSpec((1, L, w), lambda b, c: (b, c, 0))
    bwd = lambda w: pl.BlockSpec((1, L, w), lambda b, c: (b, nc - 1 - c, 0))
    n_chains = 2 * N_MLSTM_HEADS
    return pl.pallas_call(
        _mlstm_kernel,
        grid=(B, nc),
        in_specs=[fwd(W), fwd(W), fwd(W), fwd(GATE_SLAB), bwd(W), bwd(W), bwd(W), bwd(GATE_SLAB)],
        out_specs=(fwd(W), bwd(W)),
        out_shape=(jax.ShapeDtypeStruct((B, S, W), BF16), jax.ShapeDtypeStruct((B, S, W), BF16)),
        scratch_shapes=[pltpu.VMEM((n_chains, MLSTM_HEAD_DIM, AUG), F32),
                        pltpu.VMEM((n_chains, 1, LANES), F32)],
        compiler_params=pltpu.CompilerParams(dimension_semantics=("arbitrary", "arbitrary"),
                                             vmem_limit_bytes=VMEM_LIMIT_BYTES),
        name="mlstm",
    )(q3, k3, v3, g3, q3, k3, v3, g3)


def _merge_kernel(x_ref, ya_ref, hf_ref, hb_ref, mo_ref, ga_ref, gm_ref, mnw_ref, wa_ref,
                  wm_ref, wo_ref, pmw_ref, pfw_ref, x1_ref, h2_ref):
    dh = MLSTM_HEAD_DIM
    tm = x_ref.shape[0]
    n_sub = MERGE_SUBTILES if tm % (MERGE_SUBTILES * 16) == 0 else 1
    subs = [pl.ds(r * (tm // n_sub), tm // n_sub) for r in range(n_sub)]
    st = [dict() for _ in subs]

    def stage_recurrent(i):
        rows, rec = subs[i], None
        for h in range(N_MLSTM_HEADS):
            hs = slice(h * dh, (h + 1) * dh)
            o = (jax.nn.sigmoid(mo_ref[rows, hs].astype(F32))
                 * (hf_ref[rows, hs].astype(F32) + hb_ref[rows, hs].astype(F32)))
            part = jnp.dot(_rms(o, mnw_ref[:, hs]).astype(BF16), wm_ref[hs, :],
                           preferred_element_type=F32)
            rec = part if rec is None else rec + part
        st[i]["rec"] = rec

    def stage_attention(i):
        rows = subs[i]
        st[i]["att"] = ga_ref[rows, :].astype(F32) * jnp.dot(
            ya_ref[rows, :], wa_ref[...], preferred_element_type=F32)

    def stage_mix(i):
        rows = subs[i]
        merged = st[i]["att"] + gm_ref[rows, :].astype(F32) * st[i]["rec"]
        st[i]["mixed"] = jnp.dot(merged.astype(BF16), wo_ref[...], preferred_element_type=F32)

    def stage_norms(i):
        rows = subs[i]
        x1 = x_ref[rows, :] + _rms(st[i]["mixed"], pmw_ref[...])
        x1_ref[rows, :] = x1
        h2_ref[rows, :] = _rms(x1, pfw_ref[...]).astype(BF16)

    _emit_staggered((stage_recurrent, stage_attention, stage_mix, stage_norms), n_sub)


def _merge(x2d, ya, hf, hb, mo, ga, gm, mnw, wa, wm, wo, pmw, pfw):
    T = x2d.shape[0]
    tm = min(ROW_TILE, T)
    assert T % tm == 0
    row = pl.BlockSpec((tm, D_MODEL), lambda i: (i, 0))
    sq = _const_spec((D_MODEL, D_MODEL))
    vec = _const_spec((1, D_MODEL))
    return pl.pallas_call(
        _merge_kernel,
        grid=(T // tm,),
        in_specs=[row, row, row, row, row, row, row, vec, sq, sq, sq, vec, vec],
        out_specs=(row, row),
        out_shape=(jax.ShapeDtypeStruct((T, D_MODEL), F32),
                   jax.ShapeDtypeStruct((T, D_MODEL), BF16)),
        compiler_params=pltpu.CompilerParams(dimension_semantics=("parallel",),
                                             vmem_limit_bytes=VMEM_LIMIT_BYTES),
        name="merge",
    )(x2d, ya, hf, hb, mo, ga, gm, mnw, wa, wm, wo, pmw, pfw)


HALO = 16


def _ffn_kernel(x1_ref, h2_ref, hp_ref, hn_ref, wu_ref, cw_ref, cb_ref, wd_ref, nw_ref,
                out_ref, f_s, *, tiles_per_seq):
    i = pl.program_id(0)
    tm = h2_ref.shape[0]
    has_prev = (i % tiles_per_seq) != 0
    has_next = (i % tiles_per_seq) != tiles_per_seq - 1
    h2 = h2_ref[...]
    h2_ext = jnp.concatenate([hp_ref[...], h2, hn_ref[...]], axis=0)
    row = lax.broadcasted_iota(jnp.int32, (tm, 1), 0)
    n_chunks = D_FF // FF_CHUNK
    st = [dict() for _ in range(n_chunks)]

    def stage_up(c):
        cs = slice(c * FF_CHUNK, (c + 1) * FF_CHUNK)
        gs = slice(D_FF + c * FF_CHUNK, D_FF + (c + 1) * FF_CHUNK)
        st[c]["a_ext"] = jnp.dot(h2_ext, wu_ref[:, cs], preferred_element_type=F32)
        st[c]["gate"] = jnp.dot(h2, wu_ref[:, gs], preferred_element_type=F32)

    def stage_act(c):
        cs = slice(c * FF_CHUNK, (c + 1) * FF_CHUNK)
        a_ext = st[c]["a_ext"]
        a = a_ext[HALO:HALO + tm]
        a_prev = jnp.where(has_prev, a_ext[HALO - 1:HALO], 0.0)
        a_next = jnp.where(has_next, a_ext[HALO + tm:HALO + tm + 1], 0.0)
        a_dn = jnp.where(row == 0, a_prev, pltpu.roll(a, 1, axis=0))
        a_up = jnp.where(row == tm - 1, a_next, pltpu.roll(a, tm - 1, axis=0))
        conv = (a_dn * cw_ref[0:1, cs] + a * cw_ref[1:2, cs] + a_up * cw_ref[2:3, cs]
                + cb_ref[:, cs])
        f_s[:, cs] = (jax.nn.gelu(conv, approximate=True) * st[c]["gate"]).astype(BF16)

    _emit_staggered((stage_up, stage_act), n_chunks)
    y = jnp.dot(f_s[...], wd_ref[...], preferred_element_type=F32)
    out_ref[...] = x1_ref[...] + _rms(y, nw_ref[...])


def _ffn(x1, h2, S, wu, cw, cb, wd, nw):
    T = x1.shape[0]
    tm = min(ROW_TILE, S)
    assert S % tm == 0 and tm % HALO == 0 and D_FF % FF_CHUNK == 0
    nt = T // tm
    hb = tm // HALO
    row = lambda dt: pl.BlockSpec((tm, D_MODEL), lambda i: (i, 0))
    return pl.pallas_call(
        functools.partial(_ffn_kernel, tiles_per_seq=S // tm),
        grid=(nt,),
        in_specs=[row(F32), row(BF16),
                  pl.BlockSpec((HALO, D_MODEL), lambda i: (jnp.maximum(i * hb - 1, 0), 0)),
                  pl.BlockSpec((HALO, D_MODEL),
                               lambda i: (jnp.minimum((i + 1) * hb, nt * hb - 1), 0)),
                  _const_spec((D_MODEL, 2 * D_FF)), _const_spec((3, D_FF)),
                  _const_spec((1, D_FF)), _const_spec((D_FF, D_MODEL)),
                  _const_spec((1, D_MODEL))],
        out_specs=row(F32),
        out_shape=jax.ShapeDtypeStruct((T, D_MODEL), F32),
        scratch_shapes=[pltpu.VMEM((tm, D_FF), BF16)],
        compiler_params=pltpu.CompilerParams(dimension_semantics=("parallel",),
                                             vmem_limit_bytes=VMEM_LIMIT_BYTES),
        name="ffn",
    )(x1, h2, h2, h2, wu, cw, cb, wd, nw)


def _pack_in_projection(w_in, b_in):
    sizes = (1024, 256, 256, 1024, 1024, 1024, 1024, 8, 8, 2048)
    pts = [0]
    for s in sizes:
        pts.append(pts[-1] + s)
    cols = lambda t, k: t[..., pts[k]:pts[k + 1]]
    hd = ATT_HEAD_DIM

    def pack(t, one):
        zeros = lambda n: jnp.zeros(t.shape[:-1] + (n,), t.dtype)
        gate = jnp.concatenate([cols(t, 7), cols(t, 8), zeros(GATE_SLAB - 16)], axis=-1)
        v = cols(t, 2)
        ones_lane = jnp.full(t.shape[:-1] + (1,), one, t.dtype)
        v_slabs = [jnp.concatenate([v[..., g * hd:(g + 1) * hd], ones_lane,
                                    zeros(V_SLAB - hd - 1)], axis=-1)
                   for g in range(N_KV_GROUPS)]
        return jnp.concatenate([cols(t, 0), cols(t, 1)] + v_slabs
                               + [cols(t, k) for k in range(3, 7)] + [gate, cols(t, 9)], axis=-1)

    return pack(w_in, 0.0).astype(BF16), pack(b_in, 1.0)[None, :]


def _encoder_layer(x, p):
    B, S, _ = x.shape
    T = B * S
    x2d = x.reshape(T, D_MODEL)
    aq, ak, av, mq, mk, mv, mo, gl, ga, gm = _inproj(x2d, p["pre_mix_norm"], p["w_in"], p["b_in"])
    ya = _attention(aq, ak, av, p["attn_sink"], B, S).reshape(T, D_MODEL)
    seq = lambda t: t.reshape(B, S, t.shape[-1])
    q3, k3, v3, g3 = seq(mq), seq(mk), seq(mv), seq(gl)
    hf, hb = _mlstm(q3, k3, v3, g3)
    x1, h2 = _merge(x2d, ya, hf.reshape(T, D_MODEL), hb.reshape(T, D_MODEL), mo, ga, gm,
                    p["mlstm_norm"], p["w_att_branch"], p["w_mlstm_branch"], p["w_out"],
                    p["post_mix_norm"], p["pre_ffn_norm"])
    y = _ffn(x1, h2, S, p["w_ffn_up"], p["ffn_conv_w"], p["ffn_conv_b"], p["w_ffn_down"],
             p["post_ffn_norm"])
    return y.reshape(B, S, D_MODEL)


def kernel(x_prompt, x_sample, pre_mix_norm, w_in, b_in, attn_sink, mlstm_norm, w_att_branch,
           w_mlstm_branch, w_out, post_mix_norm, pre_ffn_norm, w_ffn_up, ffn_conv_w, ffn_conv_b,
           w_ffn_down, post_ffn_norm):
    depth = w_in.shape[0]
    layers = []
    for l in range(depth):
        w_packed, b_packed = _pack_in_projection(w_in[l], b_in[l])
        layers.append(dict(
            pre_mix_norm=pre_mix_norm[l][None, :], w_in=w_packed, b_in=b_packed,
            attn_sink=attn_sink[l], mlstm_norm=mlstm_norm[l][None, :],
            w_att_branch=w_att_branch[l].astype(BF16),
            w_mlstm_branch=w_mlstm_branch[l].astype(BF16), w_out=w_out[l].astype(BF16),
            post_mix_norm=post_mix_norm[l][None, :], pre_ffn_norm=pre_ffn_norm[l][None, :],
            w_ffn_up=w_ffn_up[l].astype(BF16), ffn_conv_w=ffn_conv_w[l],
            ffn_conv_b=ffn_conv_b[l][None, :], w_ffn_down=w_ffn_down[l].astype(BF16),
            post_ffn_norm=post_ffn_norm[l][None, :]))

    def trunk(x):
        for p in layers:
            x = _encoder_layer(x, p)
        return x

    return (trunk(x_prompt), trunk(x_sample))
```

```python
import functools

import jax
import jax.numpy as jnp
from jax import lax
from jax.experimental import pallas as pl
from jax.experimental.pallas import tpu as pltpu

F32 = jnp.float32
BF16 = jnp.bfloat16

D_MODEL = 1024
N_ATT_HEADS = 16
N_KV_GROUPS = 4
ATT_HEAD_DIM = 64
HEADS_PER_GROUP = N_ATT_HEADS // N_KV_GROUPS
KV_WIDTH = N_KV_GROUPS * ATT_HEAD_DIM
WINDOW = 128
BLOCK = 128
N_MLSTM_HEADS = 4
MLSTM_HEAD_DIM = 256
FORGET_BIAS = 3.0
D_FF = 2816
EPS = 1e-6
MASK_VALUE = -1e30
LOG2E = 1.4426950408889634

LANES = 128
GATE_SLAB = LANES
V_SLAB = LANES
VMEM_LIMIT_BYTES = 56 * 1024 * 1024

_SLABS = {}
_off = 0
for _name, _width in (("aq", 1024), ("ak", 256), ("av", N_KV_GROUPS * V_SLAB),
                      ("mq", 1024), ("mk", 1024),
                      ("mv", 1024), ("mo", 1024), ("gl", GATE_SLAB), ("ga", 1024), ("gm", 1024)):
    _SLABS[_name] = (_off, _width)
    _off += _width
PACKED_IN_WIDTH = _off

ROW_TILE = 512
FFN_ROW_TILE = 1024
MERGE_SUBTILES = 2
ATTN_BLOCKS_PER_STEP = 2
FFN_DOWN_SUBTILES = 2
FF_CHUNK = 256


def _const_spec(shape):
    nd = len(shape)
    return pl.BlockSpec(shape, lambda *_: (0,) * nd, pipeline_mode=pl.Buffered(1))


def _rms(x, w):
    return x * lax.rsqrt(jnp.mean(x * x, axis=-1, keepdims=True) + EPS) * w


def _emit_staggered(stages, n):
    for step in range(n + len(stages) - 1):
        for depth, stage in enumerate(stages):
            if 0 <= step - depth < n:
                stage(step - depth)


def _inproj_kernel(x_ref, nw_ref, w_ref, b_ref, aq_ref, ak_ref, av_ref, mqkv_ref, mo_ref,
                   gl_ref, gates_ref):
    h = _rms(x_ref[...], nw_ref[...]).astype(BF16)
    D = D_MODEL

    def proj(name):
        s, w = _SLABS[name]
        return jnp.dot(h, w_ref[:, s:s + w], preferred_element_type=F32) + b_ref[:, s:s + w]

    aq_ref[...] = (proj("aq") * (ATT_HEAD_DIM ** -0.5 * LOG2E)).astype(BF16)
    ak_ref[...] = proj("ak").astype(BF16)
    av_ref[...] = proj("av").astype(BF16)
    mqkv_ref[:, 0:D] = proj("mq").astype(BF16)
    mqkv_ref[:, D:2 * D] = (proj("mk") * (MLSTM_HEAD_DIM ** -0.5)).astype(BF16)
    mqkv_ref[:, 2 * D:3 * D] = proj("mv").astype(BF16)
    mo_ref[...] = proj("mo").astype(BF16)
    gl_ref[...] = proj("gl")
    gates_ref[:, 0:D] = jax.nn.sigmoid(proj("ga")).astype(BF16)
    gates_ref[:, D:2 * D] = jax.nn.sigmoid(proj("gm")).astype(BF16)


def _inproj(x2d, norm_w, w_packed, b_packed):
    T = x2d.shape[0]
    tm = min(ROW_TILE, T)
    assert T % tm == 0
    row = lambda w: pl.BlockSpec((tm, w), lambda i: (i, 0))
    outs = ((_SLABS["aq"][1], BF16), (_SLABS["ak"][1], BF16), (_SLABS["av"][1], BF16),
            (3 * D_MODEL, BF16), (D_MODEL, BF16), (GATE_SLAB, F32), (2 * D_MODEL, BF16))
    return pl.pallas_call(
        _inproj_kernel,
        grid=(T // tm,),
        in_specs=[row(D_MODEL), _const_spec((1, D_MODEL)),
                  _const_spec((D_MODEL, PACKED_IN_WIDTH)), _const_spec((1, PACKED_IN_WIDTH))],
        out_specs=tuple(row(w) for w, _ in outs),
        out_shape=tuple(jax.ShapeDtypeStruct((T, w), dt) for w, dt in outs),
        compiler_params=pltpu.CompilerParams(dimension_semantics=("parallel",),
                                             vmem_limit_bytes=VMEM_LIMIT_BYTES),
        name="inproj",
    )(x2d, norm_w, w_packed, b_packed)


def _attn_kernel(sink_ref, q_ref, k_ref, v_ref, bias_ref, o_ref, *, nb):
    hd = ATT_HEAD_DIM
    nt = (((1,), (1,)), ((), ()))
    work = []
    for sub in range(ATTN_BLOCKS_PER_STEP):
        j = pl.program_id(1) * ATTN_BLOCKS_PER_STEP + sub
        variant = jnp.where(j == 0, 0, jnp.where(j == nb - 1, 2, 1))
        kv_rows = [pl.ds(pl.multiple_of(blk * BLOCK, BLOCK), BLOCK)
                   for blk in (jnp.maximum(j - 1, 0), j, jnp.minimum(j + 1, nb - 1))]
        work.append((slice(sub * BLOCK, (sub + 1) * BLOCK), kv_rows, variant))
    items = [(w, g) for w in work for g in range(N_KV_GROUPS)]
    scores = [None] * len(items)

    def stage_scores(i):
        (q_rows, kv_rows, _), g = items[i]
        kg = jnp.concatenate([k_ref[0, r, g * hd:(g + 1) * hd] for r in kv_rows], axis=0)
        qs = jnp.concatenate([q_ref[0, q_rows, (g * HEADS_PER_GROUP + r) * hd:
                                    (g * HEADS_PER_GROUP + r + 1) * hd]
                              for r in range(HEADS_PER_GROUP)], axis=0)
        scores[i] = lax.dot_general(qs, kg, nt, preferred_element_type=F32)

    def stage_softmax_pv(i):
        (q_rows, kv_rows, variant), g = items[i]
        s = scores[i]
        vg = jnp.concatenate([v_ref[0, r, g * V_SLAB:(g + 1) * V_SLAB] for r in kv_rows],
                             axis=0)
        ps, sink_terms = [], []
        for r in range(HEADS_PER_GROUP):
            h = g * HEADS_PER_GROUP + r
            sh = s[r * BLOCK:(r + 1) * BLOCK] + bias_ref[variant, h]
            sink = sink_ref[h] * LOG2E
            m = jnp.maximum(jnp.max(sh, axis=-1, keepdims=True), sink)
            ps.append(jnp.exp2(sh - m).astype(BF16))
            sink_terms.append(jnp.exp2(sink - m))
        o = jnp.dot(jnp.concatenate(ps, axis=0), vg, preferred_element_type=F32)
        outs = []
        for r in range(HEADS_PER_GROUP):
            o_r = o[r * BLOCK:(r + 1) * BLOCK]
            outs.append(o_r[:, :hd] / (o_r[:, hd:hd + 1] + sink_terms[r]))
        gs = slice(g * HEADS_PER_GROUP * hd, (g + 1) * HEADS_PER_GROUP * hd)
        o_ref[0, q_rows, gs] = jnp.concatenate(outs, axis=1).astype(BF16)

    for i in range(len(items)):
        stage_scores(i)
    for i in range(len(items)):
        stage_softmax_pv(i)


def _attention_bias():
    rel = (jnp.arange(3 * BLOCK)[None, :] - BLOCK) - jnp.arange(BLOCK)[:, None]
    slopes = 2.0 ** (-8.0 * jnp.arange(1, N_ATT_HEADS + 1, dtype=F32) / N_ATT_HEADS)
    alibi = -slopes[:, None, None] * jnp.abs(rel).astype(F32)[None] * LOG2E
    band = (jnp.abs(rel) <= WINDOW)[None]
    col = jnp.arange(3 * BLOCK)[None, None, :]
    variants = []
    for has_prev, has_next in ((False, True), (True, True), (True, False)):
        ok = band & ((col >= BLOCK) | has_prev) & ((col < 2 * BLOCK) | has_next)
        variants.append(jnp.where(ok, alibi, MASK_VALUE))
    return jnp.stack(variants)


def _attention(aq, ak, av, sink, B, S):
    nb = S // BLOCK
    qrows = ATTN_BLOCKS_PER_STEP * BLOCK
    assert S % qrows == 0 and nb >= 2
    q3 = aq.reshape(B, S, N_ATT_HEADS * ATT_HEAD_DIM)
    k3 = ak.reshape(B, S, KV_WIDTH)
    v3 = av.reshape(B, S, N_KV_GROUPS * V_SLAB)
    whole_seq = lambda w: pl.BlockSpec((1, S, w), lambda b, j: (b, 0, 0))
    return pl.pallas_call(
        functools.partial(_attn_kernel, nb=nb),
        grid=(B, S // qrows),
        in_specs=[pl.BlockSpec(memory_space=pltpu.SMEM),
                  pl.BlockSpec((1, qrows, D_MODEL), lambda b, j: (b, j, 0)),
                  whole_seq(KV_WIDTH), whole_seq(N_KV_GROUPS * V_SLAB),
                  _const_spec((3, N_ATT_HEADS, BLOCK, 3 * BLOCK))],
        out_specs=pl.BlockSpec((1, qrows, D_MODEL), lambda b, j: (b, j, 0)),
        out_shape=jax.ShapeDtypeStruct((B, S, D_MODEL), BF16),
        compiler_params=pltpu.CompilerParams(dimension_semantics=("arbitrary", "arbitrary"),
                                             vmem_limit_bytes=VMEM_LIMIT_BYTES),
        name="attention",
    )(sink, q3, k3, v3, _attention_bias())


MLSTM_CHUNK = 256
AUG = MLSTM_HEAD_DIM + LANES


def _lanes(x, tiles):
    return jnp.concatenate([x] * tiles, axis=1)


def _gate_vectors(g_ref, reverse):
    L, nh = MLSTM_CHUNK, N_MLSTM_HEADS
    gl_raw = g_ref[0]
    gl = gl_raw * LOG2E
    lf_all = jax.nn.log_sigmoid(gl_raw + FORGET_BIAS) * LOG2E
    t_idx = lax.broadcasted_iota(jnp.int32, (L, L), 0)
    s_idx = lax.broadcasted_iota(jnp.int32, (L, L), 1)
    seen = (s_idx >= t_idx) if reverse else (s_idx <= t_idx)
    unseen_bias = jnp.where(seen, 0.0, -jnp.inf)
    seen_bf = seen.astype(BF16)
    lf_hi = lf_all.astype(BF16)
    lf_r = lf_all - lf_hi.astype(F32)
    lf_mid = lf_r.astype(BF16)
    lf_lo = (lf_r - lf_mid.astype(F32)).astype(BF16)
    bcol_all = (jnp.dot(seen_bf, lf_hi, preferred_element_type=F32)
                + jnp.dot(seen_bf, lf_mid, preferred_element_type=F32)
                + jnp.dot(seen_bf, lf_lo, preferred_element_type=F32))
    brow_all = bcol_all.T
    lirow_all = gl.T
    direction = 1 if reverse else 0
    last = 0 if reverse else L - 1
    per_head = []
    for h in range(nh):
        ci = direction * nh + h
        cf = 2 * nh + ci
        b_rep = jnp.broadcast_to(bcol_all[:, cf:cf + 1], (L, LANES))
        per_head.append(dict(b=b_rep,
                             li=jnp.broadcast_to(gl[:, ci:ci + 1], (L, LANES)),
                             u_row=lirow_all[ci:ci + 1, :] - brow_all[cf:cf + 1, :],
                             b_last=b_rep[last:last + 1, :]))
    return unseen_bias, per_head


def _mlstm_kernel(xf_ref, gf_ref, xb_ref, gb_ref, hf_ref, hb_ref, c_s, m_s):
    L, dh, nh = MLSTM_CHUNK, MLSTM_HEAD_DIM, N_MLSTM_HEADS
    W = nh * dh
    n_aug = AUG // LANES

    @pl.when(pl.program_id(1) == 0)
    def _():
        c_s[...] = jnp.zeros_like(c_s)
        m_s[...] = jnp.zeros_like(m_s)

    ones_block = jnp.ones((L, LANES), BF16)
    nt = (((1,), (1,)), ((), ()))
    tn = (((0,), (0,)), ((), ()))

    chains = []
    for direction, (x_ref, o_ref) in enumerate(((xf_ref, hf_ref), (xb_ref, hb_ref))):
        for h in range(nh):
            cols = [slice(part * W + h * dh, part * W + (h + 1) * dh) for part in range(3)]
            chains.append((x_ref, o_ref, cols, cols[0], direction * nh + h))
    n = len(chains)
    st = [dict() for _ in range(n)]

    mask_f, gates_f = _gate_vectors(gf_ref, reverse=False)
    mask_b, gates_b = _gate_vectors(gb_ref, reverse=True)
    gates = gates_f + gates_b
    masks = [mask_f] * nh + [mask_b] * nh

    for i, (x_ref, _, (qs, ks, vs), _, slot) in enumerate(chains):
        s = st[i]
        s["c_prev"] = c_s[slot]
        s["m_prev"] = m_s[slot]
        s["v_aug"] = jnp.concatenate([x_ref[0, :, vs], ones_block], axis=1)
        s["qk"] = lax.dot_general(x_ref[0, :, qs], x_ref[0, :, ks], nt,
                                  preferred_element_type=F32)
        s["inter"] = jnp.dot(x_ref[0, :, qs], s["c_prev"].astype(BF16),
                             preferred_element_type=F32)

    for i, (x_ref, _, (_, ks, _), _, slot) in enumerate(chains):
        s, gv = st[i], gates[i]
        g_rep = gv["b_last"] - gv["b"] + gv["li"]
        m_new = jnp.maximum(gv["b_last"] + s["m_prev"], jnp.max(g_rep, axis=0, keepdims=True))
        decay = jnp.exp2(gv["b_last"] + s["m_prev"] - m_new)
        wk = (_lanes(jnp.exp2(g_rep - m_new), dh // LANES)
              * x_ref[0, :, ks].astype(F32)).astype(BF16)
        c_s[slot] = _lanes(decay, n_aug) * s["c_prev"] + lax.dot_general(
            wk, s["v_aug"], tn, preferred_element_type=F32)
        m_s[slot] = m_new

    for i in range(n):
        s, gv = st[i], gates[i]
        d = (_lanes(gv["b"], L // LANES) + gv["u_row"]) + masks[i]
        s["m_t"] = jnp.maximum(gv["b"] + s["m_prev"], jnp.max(d, axis=-1, keepdims=True))
        a = (s["qk"] * jnp.exp2(d - _lanes(s["m_t"], L // LANES))).astype(BF16)
        s["intra"] = jnp.dot(a, s["v_aug"], preferred_element_type=F32)

    for i, (_, o_ref, _, hs, _) in enumerate(chains):
        s, gv = st[i], gates[i]
        w_inter = jnp.exp2(gv["b"] + s["m_prev"] - s["m_t"])
        numden = _lanes(w_inter, n_aug) * s["inter"] + s["intra"]
        den = numden[:, dh:]
        scale = 1.0 / jnp.maximum(jnp.abs(den), jnp.exp2(-s["m_t"]))
        o_ref[0, :, hs] = (numden[:, :dh] * _lanes(scale, dh // LANES)).astype(o_ref.dtype)


def _mlstm(qkv3, g3):
    B, S, W3 = qkv3.shape
    W = W3 // 3
    L = MLSTM_CHUNK
    nc = S // L
    assert S % L == 0 and W == N_MLSTM_HEADS * MLSTM_HEAD_DIM
    fwd = lambda w: pl.BlockSpec((1, L, w), lambda b, c: (b, c, 0))
    bwd = lambda w: pl.BlockSpec((1, L, w), lambda b, c: (b, nc - 1 - c, 0))
    n_chains = 2 * N_MLSTM_HEADS
    return pl.pallas_call(
        _mlstm_kernel,
        grid=(B, nc),
        in_specs=[fwd(W3), fwd(GATE_SLAB), bwd(W3), bwd(GATE_SLAB)],
        out_specs=(fwd(W), bwd(W)),
        out_shape=(jax.ShapeDtypeStruct((B, S, W), BF16), jax.ShapeDtypeStruct((B, S, W), BF16)),
        scratch_shapes=[pltpu.VMEM((n_chains, MLSTM_HEAD_DIM, AUG), F32),
                        pltpu.VMEM((n_chains, 1, LANES), F32)],
        compiler_params=pltpu.CompilerParams(dimension_semantics=("arbitrary", "arbitrary"),
                                             vmem_limit_bytes=VMEM_LIMIT_BYTES),
        name="mlstm",
    )(qkv3, g3, qkv3, g3)


def _merge_kernel(x_ref, ya_ref, hf_ref, hb_ref, mo_ref, gates_ref, mnw_ref, wa_ref, wm_ref,
                  wo_ref, pmw_ref, pfw_ref, x1_ref, h2_ref):
    dh = MLSTM_HEAD_DIM
    tm = x_ref.shape[0]
    n_sub = MERGE_SUBTILES if tm % (MERGE_SUBTILES * 16) == 0 else 1
    subs = [pl.ds(r * (tm // n_sub), tm // n_sub) for r in range(n_sub)]
    st = [dict() for _ in subs]

    def stage_recurrent(i):
        rows, rec = subs[i], None
        for h in range(N_MLSTM_HEADS):
            hs = slice(h * dh, (h + 1) * dh)
            o = (jax.nn.sigmoid(mo_ref[rows, hs].astype(F32))
                 * (hf_ref[rows, hs].astype(F32) + hb_ref[rows, hs].astype(F32)))
            part = jnp.dot(_rms(o, mnw_ref[:, hs]).astype(BF16), wm_ref[hs, :],
                           preferred_element_type=F32)
            rec = part if rec is None else rec + part
        st[i]["rec"] = rec

    def stage_attention(i):
        rows = subs[i]
        st[i]["att"] = gates_ref[rows, :D_MODEL].astype(F32) * jnp.dot(
            ya_ref[rows, :], wa_ref[...], preferred_element_type=F32)

    def stage_mix(i):
        rows = subs[i]
        merged = st[i]["att"] + gates_ref[rows, D_MODEL:].astype(F32) * st[i]["rec"]
        st[i]["mixed"] = jnp.dot(merged.astype(BF16), wo_ref[...], preferred_element_type=F32)

    def stage_norms(i):
        rows = subs[i]
        x1 = x_ref[rows, :] + _rms(st[i]["mixed"], pmw_ref[...])
        x1_ref[rows, :] = x1
        h2_ref[rows, :] = _rms(x1, pfw_ref[...]).astype(BF16)

    _emit_staggered((stage_recurrent, stage_attention, stage_mix, stage_norms), n_sub)


def _merge(x2d, ya, hf, hb, mo, gates, mnw, wa, wm, wo, pmw, pfw):
    T = x2d.shape[0]
    tm = min(ROW_TILE, T)
    assert T % tm == 0
    row = pl.BlockSpec((tm, D_MODEL), lambda i: (i, 0))
    sq = _const_spec((D_MODEL, D_MODEL))
    vec = _const_spec((1, D_MODEL))
    return pl.pallas_call(
        _merge_kernel,
        grid=(T // tm,),
        in_specs=[row, row, row, row, row, pl.BlockSpec((tm, 2 * D_MODEL), lambda i: (i, 0)),
                  vec, sq, sq, sq, vec, vec],
        out_specs=(row, row),
        out_shape=(jax.ShapeDtypeStruct((T, D_MODEL), F32),
                   jax.ShapeDtypeStruct((T, D_MODEL), BF16)),
        compiler_params=pltpu.CompilerParams(dimension_semantics=("parallel",),
                                             vmem_limit_bytes=VMEM_LIMIT_BYTES),
        name="merge",
    )(x2d, ya, hf, hb, mo, gates, mnw, wa, wm, wo, pmw, pfw)


HALO = 16


def _ffn_kernel(x1_ref, h2_ref, hp_ref, hn_ref, wu_ref, cw_ref, cb_ref, wd_ref, nw_ref,
                out_ref, f_s, *, tiles_per_seq):
    i = pl.program_id(0)
    tm = h2_ref.shape[0]
    has_prev = (i % tiles_per_seq) != 0
    has_next = (i % tiles_per_seq) != tiles_per_seq - 1
    h2 = h2_ref[...]
    h2_ext = jnp.concatenate([hp_ref[...], h2, hn_ref[...]], axis=0)
    row = lax.broadcasted_iota(jnp.int32, (tm, 1), 0)
    n_chunks = D_FF // FF_CHUNK
    st = [dict() for _ in range(n_chunks)]

    def stage_up(c):
        cs = slice(c * FF_CHUNK, (c + 1) * FF_CHUNK)
        gs = slice(D_FF + c * FF_CHUNK, D_FF + (c + 1) * FF_CHUNK)
        st[c]["a_ext"] = jnp.dot(h2_ext, wu_ref[:, cs], preferred_element_type=F32)
        st[c]["gate"] = jnp.dot(h2, wu_ref[:, gs], preferred_element_type=F32)

    def stage_act(c):
        cs = slice(c * FF_CHUNK, (c + 1) * FF_CHUNK)
        a_ext = st[c]["a_ext"]
        a = a_ext[HALO:HALO + tm]
        a_prev = jnp.where(has_prev, a_ext[HALO - 1:HALO], 0.0)
        a_next = jnp.where(has_next, a_ext[HALO + tm:HALO + tm + 1], 0.0)
        a_dn = jnp.where(row == 0, a_prev, pltpu.roll(a, 1, axis=0))
        a_up = jnp.where(row == tm - 1, a_next, pltpu.roll(a, tm - 1, axis=0))
        conv = (a_dn * cw_ref[0:1, cs] + a * cw_ref[1:2, cs] + a_up * cw_ref[2:3, cs]
                + cb_ref[:, cs])
        f_s[:, cs] = (jax.nn.gelu(conv, approximate=True) * st[c]["gate"]).astype(BF16)

    _emit_staggered((stage_up, stage_act), n_chunks)
    n_sub = FFN_DOWN_SUBTILES if tm % (FFN_DOWN_SUBTILES * 16) == 0 else 1
    subs = [pl.ds(r * (tm // n_sub), tm // n_sub) for r in range(n_sub)]
    ys = [jnp.dot(f_s[rows, :], wd_ref[...], preferred_element_type=F32) for rows in subs]
    for rows, y in zip(subs, ys):
        out_ref[rows, :] = x1_ref[rows, :] + _rms(y, nw_ref[...])


def _ffn(x1, h2, S, wu, cw, cb, wd, nw):
    T = x1.shape[0]
    tm = min(FFN_ROW_TILE, S)
    assert S % tm == 0 and tm % HALO == 0 and D_FF % FF_CHUNK == 0
    nt = T // tm
    hb = tm // HALO
    row = lambda dt: pl.BlockSpec((tm, D_MODEL), lambda i: (i, 0))
    return pl.pallas_call(
        functools.partial(_ffn_kernel, tiles_per_seq=S // tm),
        grid=(nt,),
        in_specs=[row(F32), row(BF16),
                  pl.BlockSpec((HALO, D_MODEL), lambda i: (jnp.maximum(i * hb - 1, 0), 0)),
                  pl.BlockSpec((HALO, D_MODEL),
                               lambda i: (jnp.minimum((i + 1) * hb, nt * hb - 1), 0)),
                  _const_spec((D_MODEL, 2 * D_FF)), _const_spec((3, D_FF)),
                  _const_spec((1, D_FF)), _const_spec((D_FF, D_MODEL)),
                  _const_spec((1, D_MODEL))],
        out_specs=row(F32),
        out_shape=jax.ShapeDtypeStruct((T, D_MODEL), F32),
        scratch_shapes=[pltpu.VMEM((tm, D_FF), BF16)],
        compiler_params=pltpu.CompilerParams(dimension_semantics=("parallel",),
                                             vmem_limit_bytes=VMEM_LIMIT_BYTES),
        name="ffn",
    )(x1, h2, h2, h2, wu, cw, cb, wd, nw)


def _pack_in_projection(w_in, b_in):
    sizes = (1024, 256, 256, 1024, 1024, 1024, 1024, 8, 8, 2048)
    pts = [0]
    for s in sizes:
        pts.append(pts[-1] + s)
    cols = lambda t, k: t[..., pts[k]:pts[k + 1]]
    hd = ATT_HEAD_DIM

    def pack(t, one):
        zeros = lambda n: jnp.zeros(t.shape[:-1] + (n,), t.dtype)
        gate = jnp.concatenate([cols(t, 7), cols(t, 8), zeros(GATE_SLAB - 16)], axis=-1)
        v = cols(t, 2)
        ones_lane = jnp.full(t.shape[:-1] + (1,), one, t.dtype)
        v_slabs = [jnp.concatenate([v[..., g * hd:(g + 1) * hd], ones_lane,
                                    zeros(V_SLAB - hd - 1)], axis=-1)
                   for g in range(N_KV_GROUPS)]
        return jnp.concatenate([cols(t, 0), cols(t, 1)] + v_slabs
                               + [cols(t, k) for k in range(3, 7)] + [gate, cols(t, 9)], axis=-1)

    return pack(w_in.astype(BF16), 0.0), pack(b_in, 1.0)[None, :]


def _encoder_layer(x, p):
    B, S, _ = x.shape
    T = B * S
    x2d = x.reshape(T, D_MODEL)
    aq, ak, av, mqkv, mo, gl, gates = _inproj(x2d, p["pre_mix_norm"], p["w_in"], p["b_in"])
    ya = _attention(aq, ak, av, p["attn_sink"], B, S).reshape(T, D_MODEL)
    seq = lambda t: t.reshape(B, S, t.shape[-1])
    hf, hb = _mlstm(seq(mqkv), seq(gl))
    x1, h2 = _merge(x2d, ya, hf.reshape(T, D_MODEL), hb.reshape(T, D_MODEL), mo, gates,
                    p["mlstm_norm"], p["w_att_branch"], p["w_mlstm_branch"], p["w_out"],
                    p["post_mix_norm"], p["pre_ffn_norm"])
    y = _ffn(x1, h2, S, p["w_ffn_up"], p["ffn_conv_w"], p["ffn_conv_b"], p["w_ffn_down"],
             p["post_ffn_norm"])
    return y.reshape(B, S, D_MODEL)


def kernel(x_prompt, x_sample, pre_mix_norm, w_in, b_in, attn_sink, mlstm_norm, w_att_branch,
           w_mlstm_branch, w_out, post_mix_norm, pre_ffn_norm, w_ffn_up, ffn_conv_w, ffn_conv_b,
           w_ffn_down, post_ffn_norm):
    depth = w_in.shape[0]
    layers = []
    for l in range(depth):
        w_packed, b_packed = _pack_in_projection(w_in[l], b_in[l])
        layers.append(dict(
            pre_mix_norm=pre_mix_norm[l][None, :], w_in=w_packed, b_in=b_packed,
            attn_sink=attn_sink[l], mlstm_norm=mlstm_norm[l][None, :],
            w_att_branch=w_att_branch[l].astype(BF16),
            w_mlstm_branch=w_mlstm_branch[l].astype(BF16), w_out=w_out[l].astype(BF16),
            post_mix_norm=post_mix_norm[l][None, :], pre_ffn_norm=pre_ffn_norm[l][None, :],
            w_ffn_up=w_ffn_up[l].astype(BF16), ffn_conv_w=ffn_conv_w[l],
            ffn_conv_b=ffn_conv_b[l][None, :], w_ffn_down=w_ffn_down[l].astype(BF16),
            post_ffn_norm=post_ffn_norm[l][None, :]))

    def trunk(x):
        for p in layers:
            x = _encoder_layer(x, p)
        return x

    return (trunk(x_prompt), trunk(x_sample))
```

```python
import functools

import jax
import jax.numpy as jnp
from jax import lax
from jax.experimental import pallas as pl
from jax.experimental.pallas import tpu as pltpu

F32 = jnp.float32
BF16 = jnp.bfloat16

D_MODEL = 1024
N_ATT_HEADS = 16
N_KV_GROUPS = 4
ATT_HEAD_DIM = 64
HEADS_PER_GROUP = N_ATT_HEADS // N_KV_GROUPS
KV_WIDTH = N_KV_GROUPS * ATT_HEAD_DIM
WINDOW = 128
BLOCK = 128
N_MLSTM_HEADS = 4
MLSTM_HEAD_DIM = 256
FORGET_BIAS = 3.0
D_FF = 2816
EPS = 1e-6
MASK_VALUE = -1e30
LOG2E = 1.4426950408889634

LANES = 128
GATE_SLAB = LANES
VMEM_LIMIT_BYTES = 56 * 1024 * 1024

_HEAD_SLABS = {"aq": (0, 1024), "ak": (1024, 256), "av": (1280, 256), "mq": (1536, 1024),
               "mk": (2560, 1024), "mv": (3584, 1024), "mo": (4608, 1024)}
HEAD_WIDTH = 5632
IN_WIDTH = HEAD_WIDTH + 16 + 2 * D_MODEL
_TAIL_SLABS = {"gl": (0, GATE_SLAB), "ga": (GATE_SLAB, 1024), "gm": (GATE_SLAB + 1024, 1024)}
TAIL_WIDTH = GATE_SLAB + 2 * D_MODEL

ROW_TILE = 512
FFN_ROW_TILE = 1024
MERGE_SUBTILES = 2
ATTN_BLOCKS_PER_STEP = 2
FFN_DOWN_SUBTILES = 2
FF_CHUNK = 256


def _const_spec(shape):
    nd = len(shape)
    return pl.BlockSpec(shape, lambda *_: (0,) * nd, pipeline_mode=pl.Buffered(1))


def _rms(x, w):
    return x * lax.rsqrt(jnp.mean(x * x, axis=-1, keepdims=True) + EPS) * w


def _emit_staggered(stages, n):
    for step in range(n + len(stages) - 1):
        for depth, stage in enumerate(stages):
            if 0 <= step - depth < n:
                stage(step - depth)


def _inproj_kernel(x_ref, nw_ref, w_ref, b_ref, wt_ref, bt_ref, aq_ref, ak_ref, av_ref, mqkv_ref,
                   mo_ref, gl_ref, gates_ref):
    h = _rms(x_ref[...], nw_ref[...]).astype(BF16)
    D = D_MODEL

    def proj(name):
        if name in _HEAD_SLABS:
            (s, w), wr, br = _HEAD_SLABS[name], w_ref, b_ref
        else:
            (s, w), wr, br = _TAIL_SLABS[name], wt_ref, bt_ref
        return jnp.dot(h, wr[:, s:s + w], preferred_element_type=F32) + br[:, s:s + w]

    aq_ref[...] = (proj("aq") * (ATT_HEAD_DIM ** -0.5 * LOG2E)).astype(BF16)
    ak_ref[...] = proj("ak").astype(BF16)
    av_ref[...] = proj("av").astype(BF16)
    mqkv_ref[:, 0:D] = proj("mq").astype(BF16)
    mqkv_ref[:, D:2 * D] = (proj("mk") * (MLSTM_HEAD_DIM ** -0.5)).astype(BF16)
    mqkv_ref[:, 2 * D:3 * D] = proj("mv").astype(BF16)
    mo_ref[...] = proj("mo").astype(BF16)
    gl_ref[...] = proj("gl")
    gates_ref[:, 0:D] = jax.nn.sigmoid(proj("ga")).astype(BF16)
    gates_ref[:, D:2 * D] = jax.nn.sigmoid(proj("gm")).astype(BF16)


def _inproj(x2d, norm_w, w, b, w_tail, b_tail):
    T = x2d.shape[0]
    tm = min(ROW_TILE, T)
    assert T % tm == 0
    row = lambda w: pl.BlockSpec((tm, w), lambda i: (i, 0))
    outs = ((_HEAD_SLABS["aq"][1], BF16), (_HEAD_SLABS["ak"][1], BF16),
            (_HEAD_SLABS["av"][1], BF16), (3 * D_MODEL, BF16), (D_MODEL, BF16),
            (GATE_SLAB, F32), (2 * D_MODEL, BF16))
    return pl.pallas_call(
        _inproj_kernel,
        grid=(T // tm,),
        in_specs=[row(D_MODEL), _const_spec((1, D_MODEL)),
                  _const_spec((D_MODEL, IN_WIDTH)), _const_spec((1, IN_WIDTH)),
                  _const_spec((D_MODEL, TAIL_WIDTH)), _const_spec((1, TAIL_WIDTH))],
        out_specs=tuple(row(w) for w, _ in outs),
        out_shape=tuple(jax.ShapeDtypeStruct((T, w), dt) for w, dt in outs),
        compiler_params=pltpu.CompilerParams(dimension_semantics=("parallel",),
                                             vmem_limit_bytes=VMEM_LIMIT_BYTES),
        name="inproj",
    )(x2d, norm_w, w, b, w_tail, b_tail)


def _attn_kernel(sink_ref, q_ref, k_ref, v_ref, bias_ref, o_ref, *, nb):
    hd = ATT_HEAD_DIM
    nt = (((1,), (1,)), ((), ()))
    v_lane = lax.broadcasted_iota(jnp.int32, (3 * BLOCK, LANES), 1)
    work = []
    for sub in range(ATTN_BLOCKS_PER_STEP):
        j = pl.program_id(1) * ATTN_BLOCKS_PER_STEP + sub
        variant = jnp.where(j == 0, 0, jnp.where(j == nb - 1, 2, 1))
        kv_rows = [pl.ds(pl.multiple_of(blk * BLOCK, BLOCK), BLOCK)
                   for blk in (jnp.maximum(j - 1, 0), j, jnp.minimum(j + 1, nb - 1))]
        work.append((slice(sub * BLOCK, (sub + 1) * BLOCK), kv_rows, variant))
    items = [(w, g) for w in work for g in range(N_KV_GROUPS)]
    scores = [None] * len(items)

    def stage_scores(i):
        (q_rows, kv_rows, _), g = items[i]
        kg = jnp.concatenate([k_ref[0, r, g * hd:(g + 1) * hd] for r in kv_rows], axis=0)
        qs = jnp.concatenate([q_ref[0, q_rows, (g * HEADS_PER_GROUP + r) * hd:
                                    (g * HEADS_PER_GROUP + r + 1) * hd]
                              for r in range(HEADS_PER_GROUP)], axis=0)
        scores[i] = lax.dot_general(qs, kg, nt, preferred_element_type=F32)

    def stage_softmax_pv(i):
        (q_rows, kv_rows, variant), g = items[i]
        s = scores[i]
        pair = jnp.concatenate([v_ref[0, r, (g // 2) * LANES:(g // 2 + 1) * LANES]
                                for r in kv_rows], axis=0)
        if g % 2 == 0:
            vals, ones_at = slice(0, hd), hd
            vg = jnp.where(v_lane < hd, pair, (v_lane == ones_at).astype(BF16))
        else:
            vals, ones_at = slice(hd, 2 * hd), 0
            vg = jnp.where(v_lane >= hd, pair, (v_lane == ones_at).astype(BF16))
        ps, sink_terms = [], []
        for r in range(HEADS_PER_GROUP):
            h = g * HEADS_PER_GROUP + r
            sh = s[r * BLOCK:(r + 1) * BLOCK] + bias_ref[variant, h]
            sink = sink_ref[h] * LOG2E
            m = jnp.maximum(jnp.max(sh, axis=-1, keepdims=True), sink)
            ps.append(jnp.exp2(sh - m).astype(BF16))
            sink_terms.append(jnp.exp2(sink - m))
        o = jnp.dot(jnp.concatenate(ps, axis=0), vg, preferred_element_type=F32)
        outs = []
        for r in range(HEADS_PER_GROUP):
            o_r = o[r * BLOCK:(r + 1) * BLOCK]
            outs.append(o_r[:, vals] / (o_r[:, ones_at:ones_at + 1] + sink_terms[r]))
        gs = slice(g * HEADS_PER_GROUP * hd, (g + 1) * HEADS_PER_GROUP * hd)
        o_ref[0, q_rows, gs] = jnp.concatenate(outs, axis=1).astype(BF16)

    for i in range(len(items)):
        stage_scores(i)
    for i in range(len(items)):
        stage_softmax_pv(i)


def _attention_bias():
    rel = (jnp.arange(3 * BLOCK)[None, :] - BLOCK) - jnp.arange(BLOCK)[:, None]
    slopes = 2.0 ** (-8.0 * jnp.arange(1, N_ATT_HEADS + 1, dtype=F32) / N_ATT_HEADS)
    alibi = -slopes[:, None, None] * jnp.abs(rel).astype(F32)[None] * LOG2E
    band = (jnp.abs(rel) <= WINDOW)[None]
    col = jnp.arange(3 * BLOCK)[None, None, :]
    variants = []
    for has_prev, has_next in ((False, True), (True, True), (True, False)):
        ok = band & ((col >= BLOCK) | has_prev) & ((col < 2 * BLOCK) | has_next)
        variants.append(jnp.where(ok, alibi, MASK_VALUE))
    return jnp.stack(variants)


def _attention(aq, ak, av, sink, B, S):
    nb = S // BLOCK
    qrows = ATTN_BLOCKS_PER_STEP * BLOCK
    assert S % qrows == 0 and nb >= 2
    q3 = aq.reshape(B, S, N_ATT_HEADS * ATT_HEAD_DIM)
    k3 = ak.reshape(B, S, KV_WIDTH)
    v3 = av.reshape(B, S, KV_WIDTH)
    whole_seq = lambda w: pl.BlockSpec((1, S, w), lambda b, j: (b, 0, 0))
    return pl.pallas_call(
        functools.partial(_attn_kernel, nb=nb),
        grid=(B, S // qrows),
        in_specs=[pl.BlockSpec(memory_space=pltpu.SMEM),
                  pl.BlockSpec((1, qrows, D_MODEL), lambda b, j: (b, j, 0)),
                  whole_seq(KV_WIDTH), whole_seq(KV_WIDTH),
                  _const_spec((3, N_ATT_HEADS, BLOCK, 3 * BLOCK))],
        out_specs=pl.BlockSpec((1, qrows, D_MODEL), lambda b, j: (b, j, 0)),
        out_shape=jax.ShapeDtypeStruct((B, S, D_MODEL), BF16),
        compiler_params=pltpu.CompilerParams(dimension_semantics=("arbitrary", "arbitrary"),
                                             vmem_limit_bytes=VMEM_LIMIT_BYTES),
        name="attention",
    )(sink, q3, k3, v3, _attention_bias())


MLSTM_CHUNK = 256
MLSTM_CHUNKS_PER_STEP = 2
AUG = MLSTM_HEAD_DIM + LANES


def _lanes(x, tiles):
    return jnp.concatenate([x] * tiles, axis=1)


def _gate_vectors(g_ref, rows, reverse):
    L, nh = MLSTM_CHUNK, N_MLSTM_HEADS
    gl_raw = g_ref[0, rows, :]
    gl = gl_raw * LOG2E
    lf_all = jax.nn.log_sigmoid(gl_raw + FORGET_BIAS) * LOG2E
    t_idx = lax.broadcasted_iota(jnp.int32, (L, L), 0)
    s_idx = lax.broadcasted_iota(jnp.int32, (L, L), 1)
    seen = (s_idx >= t_idx) if reverse else (s_idx <= t_idx)
    unseen_bias = jnp.where(seen, 0.0, -jnp.inf)
    seen_bf = seen.astype(BF16)
    lf_hi = lf_all.astype(BF16)
    lf_r = lf_all - lf_hi.astype(F32)
    lf_mid = lf_r.astype(BF16)
    lf_lo = (lf_r - lf_mid.astype(F32)).astype(BF16)
    bcol_all = (jnp.dot(seen_bf, lf_hi, preferred_element_type=F32)
                + jnp.dot(seen_bf, lf_mid, preferred_element_type=F32)
                + jnp.dot(seen_bf, lf_lo, preferred_element_type=F32))
    brow_all = bcol_all.T
    lirow_all = gl.T
    direction = 1 if reverse else 0
    last = 0 if reverse else L - 1
    per_head = []
    for h in range(nh):
        ci = direction * nh + h
        cf = 2 * nh + ci
        b_rep = jnp.broadcast_to(bcol_all[:, cf:cf + 1], (L, LANES))
        per_head.append(dict(b=b_rep,
                             li=jnp.broadcast_to(gl[:, ci:ci + 1], (L, LANES)),
                             u_row=lirow_all[ci:ci + 1, :] - brow_all[cf:cf + 1, :],
                             b_last=b_rep[last:last + 1, :]))
    return unseen_bias, per_head


def _mlstm_kernel(xf_ref, gf_ref, xb_ref, gb_ref, hf_ref, hb_ref, c_s, m_s):
    @pl.when(pl.program_id(1) == 0)
    def _():
        c_s[...] = jnp.zeros_like(c_s)
        m_s[...] = jnp.zeros_like(m_s)

    def chunk_pair(it, carry):
        L = MLSTM_CHUNK
        rows_f = pl.ds(pl.multiple_of(it * L, L), L)
        rows_b = pl.ds(pl.multiple_of((MLSTM_CHUNKS_PER_STEP - 1 - it) * L, L), L)
        _mlstm_chunk_pair((xf_ref, gf_ref, hf_ref, rows_f), (xb_ref, gb_ref, hb_ref, rows_b),
                          c_s, m_s)
        return carry

    lax.fori_loop(0, MLSTM_CHUNKS_PER_STEP, chunk_pair, 0)


def _mlstm_chunk_pair(fwd, bwd, c_s, m_s):
    L, dh, nh = MLSTM_CHUNK, MLSTM_HEAD_DIM, N_MLSTM_HEADS
    W = nh * dh
    n_aug = AUG // LANES
    ones_block = jnp.ones((L, LANES), BF16)
    nt = (((1,), (1,)), ((), ()))
    tn = (((0,), (0,)), ((), ()))

    chains = []
    for direction, (x_ref, _, o_ref, rows) in enumerate((fwd, bwd)):
        for h in range(nh):
            cols = [slice(part * W + h * dh, part * W + (h + 1) * dh) for part in range(3)]
            chains.append((x_ref, o_ref, rows, cols, cols[0], direction * nh + h))
    n = len(chains)
    st = [dict() for _ in range(n)]

    mask_f, gates_f = _gate_vectors(fwd[1], fwd[3], reverse=False)
    mask_b, gates_b = _gate_vectors(bwd[1], bwd[3], reverse=True)
    gates = gates_f + gates_b
    masks = [mask_f] * nh + [mask_b] * nh

    for i, (x_ref, _, rows, (qs, ks, vs), _, slot) in enumerate(chains):
        s = st[i]
        s["c_prev"] = c_s[slot]
        s["m_prev"] = m_s[slot]
        s["v_aug"] = jnp.concatenate([x_ref[0, rows, vs], ones_block], axis=1)
        s["qk"] = lax.dot_general(x_ref[0, rows, qs], x_ref[0, rows, ks], nt,
                                  preferred_element_type=F32)
        s["inter"] = jnp.dot(x_ref[0, rows, qs], s["c_prev"].astype(BF16),
                             preferred_element_type=F32)

    for i, (x_ref, _, rows, (_, ks, _), _, slot) in enumerate(chains):
        s, gv = st[i], gates[i]
        g_rep = gv["b_last"] - gv["b"] + gv["li"]
        m_new = jnp.maximum(gv["b_last"] + s["m_prev"], jnp.max(g_rep, axis=0, keepdims=True))
        decay = jnp.exp2(gv["b_last"] + s["m_prev"] - m_new)
        wk = (_lanes(jnp.exp2(g_rep - m_new), dh // LANES)
              * x_ref[0, rows, ks].astype(F32)).astype(BF16)
        c_s[slot] = _lanes(decay, n_aug) * s["c_prev"] + lax.dot_general(
            wk, s["v_aug"], tn, preferred_element_type=F32)
        m_s[slot] = m_new

    for i in range(n):
        s, gv = st[i], gates[i]
        d = (_lanes(gv["b"], L // LANES) + gv["u_row"]) + masks[i]
        s["m_t"] = jnp.maximum(gv["b"] + s["m_prev"], jnp.max(d, axis=-1, keepdims=True))
        a = (s["qk"] * jnp.exp2(d - _lanes(s["m_t"], L // LANES))).astype(BF16)
        s["intra"] = jnp.dot(a, s["v_aug"], preferred_element_type=F32)

    for i, (_, o_ref, rows, _, hs, _) in enumerate(chains):
        s, gv = st[i], gates[i]
        w_inter = jnp.exp2(gv["b"] + s["m_prev"] - s["m_t"])
        numden = _lanes(w_inter, n_aug) * s["inter"] + s["intra"]
        den = numden[:, dh:]
        scale = 1.0 / jnp.maximum(jnp.abs(den), jnp.exp2(-s["m_t"]))
        o_ref[0, rows, hs] = (numden[:, :dh] * _lanes(scale, dh // LANES)).astype(o_ref.dtype)


def _mlstm(qkv3, g3):
    B, S, W3 = qkv3.shape
    W = W3 // 3
    L = MLSTM_CHUNKS_PER_STEP * MLSTM_CHUNK
    nc = S // L
    assert S % L == 0 and W == N_MLSTM_HEADS * MLSTM_HEAD_DIM
    fwd = lambda w: pl.BlockSpec((1, L, w), lambda b, c: (b, c, 0))
    bwd = lambda w: pl.BlockSpec((1, L, w), lambda b, c: (b, nc - 1 - c, 0))
    n_chains = 2 * N_MLSTM_HEADS
    return pl.pallas_call(
        _mlstm_kernel,
        grid=(B, nc),
        in_specs=[fwd(W3), fwd(GATE_SLAB), bwd(W3), bwd(GATE_SLAB)],
        out_specs=(fwd(W), bwd(W)),
        out_shape=(jax.ShapeDtypeStruct((B, S, W), BF16), jax.ShapeDtypeStruct((B, S, W), BF16)),
        scratch_shapes=[pltpu.VMEM((n_chains, MLSTM_HEAD_DIM, AUG), F32),
                        pltpu.VMEM((n_chains, 1, LANES), F32)],
        compiler_params=pltpu.CompilerParams(dimension_semantics=("arbitrary", "arbitrary"),
                                             vmem_limit_bytes=VMEM_LIMIT_BYTES),
        name="mlstm",
    )(qkv3, g3, qkv3, g3)


def _merge_kernel(x_ref, ya_ref, hf_ref, hb_ref, mo_ref, gates_ref, mnw_ref, wa_ref, wm_ref,
                  wo_ref, pmw_ref, pfw_ref, x1_ref, h2_ref):
    dh = MLSTM_HEAD_DIM
    tm = x_ref.shape[0]
    n_sub = MERGE_SUBTILES if tm % (MERGE_SUBTILES * 16) == 0 else 1
    subs = [pl.ds(r * (tm // n_sub), tm // n_sub) for r in range(n_sub)]
    st = [dict() for _ in subs]

    def stage_recurrent(i):
        rows, rec = subs[i], None
        for h in range(N_MLSTM_HEADS):
            hs = slice(h * dh, (h + 1) * dh)
            o = (jax.nn.sigmoid(mo_ref[rows, hs].astype(F32))
                 * (hf_ref[rows, hs].astype(F32) + hb_ref[rows, hs].astype(F32)))
            part = jnp.dot(_rms(o, mnw_ref[:, hs]).astype(BF16), wm_ref[hs, :],
                           preferred_element_type=F32)
            rec = part if rec is None else rec + part
        st[i]["rec"] = rec

    def stage_attention(i):
        rows = subs[i]
        st[i]["att"] = gates_ref[rows, :D_MODEL].astype(F32) * jnp.dot(
            ya_ref[rows, :], wa_ref[...], preferred_element_type=F32)

    def stage_mix(i):
        rows = subs[i]
        merged = st[i]["att"] + gates_ref[rows, D_MODEL:].astype(F32) * st[i]["rec"]
        st[i]["mixed"] = jnp.dot(merged.astype(BF16), wo_ref[...], preferred_element_type=F32)

    def stage_norms(i):
        rows = subs[i]
        x1 = x_ref[rows, :] + _rms(st[i]["mixed"], pmw_ref[...])
        x1_ref[rows, :] = x1
        h2_ref[rows, :] = _rms(x1, pfw_ref[...]).astype(BF16)

    _emit_staggered((stage_recurrent, stage_attention, stage_mix, stage_norms), n_sub)


def _merge(x2d, ya, hf, hb, mo, gates, mnw, wa, wm, wo, pmw, pfw):
    T = x2d.shape[0]
    tm = min(ROW_TILE, T)
    assert T % tm == 0
    row = pl.BlockSpec((tm, D_MODEL), lambda i: (i, 0))
    sq = _const_spec((D_MODEL, D_MODEL))
    vec = _const_spec((1, D_MODEL))
    return pl.pallas_call(
        _merge_kernel,
        grid=(T // tm,),
        in_specs=[row, row, row, row, row, pl.BlockSpec((tm, 2 * D_MODEL), lambda i: (i, 0)),
                  vec, sq, sq, sq, vec, vec],
        out_specs=(row, row),
        out_shape=(jax.ShapeDtypeStruct((T, D_MODEL), F32),
                   jax.ShapeDtypeStruct((T, D_MODEL), BF16)),
        compiler_params=pltpu.CompilerParams(dimension_semantics=("parallel",),
                                             vmem_limit_bytes=VMEM_LIMIT_BYTES),
        name="merge",
    )(x2d, ya, hf, hb, mo, gates, mnw, wa, wm, wo, pmw, pfw)


HALO = 16


def _ffn_kernel(x1_ref, h2_ref, hp_ref, hn_ref, wu_ref, cw_ref, cb_ref, wd_ref, nw_ref,
                out_ref, f_s, *, tiles_per_seq):
    i = pl.program_id(0)
    tm = h2_ref.shape[0]
    has_prev = (i % tiles_per_seq) != 0
    has_next = (i % tiles_per_seq) != tiles_per_seq - 1
    h2 = h2_ref[...]
    h2_ext = jnp.concatenate([hp_ref[...], h2, hn_ref[...]], axis=0)
    row = lax.broadcasted_iota(jnp.int32, (tm, 1), 0)
    n_chunks = D_FF // FF_CHUNK
    st = [dict() for _ in range(n_chunks)]

    def stage_up(c):
        cs = slice(c * FF_CHUNK, (c + 1) * FF_CHUNK)
        gs = slice(D_FF + c * FF_CHUNK, D_FF + (c + 1) * FF_CHUNK)
        st[c]["a_ext"] = jnp.dot(h2_ext, wu_ref[:, cs], preferred_element_type=F32)
        st[c]["gate"] = jnp.dot(h2, wu_ref[:, gs], preferred_element_type=F32)

    def stage_act(c):
        cs = slice(c * FF_CHUNK, (c + 1) * FF_CHUNK)
        a_ext = st[c]["a_ext"]
        a = a_ext[HALO:HALO + tm]
        a_prev = jnp.where(has_prev, a_ext[HALO - 1:HALO], 0.0)
        a_next = jnp.where(has_next, a_ext[HALO + tm:HALO + tm + 1], 0.0)
        a_dn = jnp.where(row == 0, a_prev, pltpu.roll(a, 1, axis=0))
        a_up = jnp.where(row == tm - 1, a_next, pltpu.roll(a, tm - 1, axis=0))
        conv = (a_dn * cw_ref[0:1, cs] + a * cw_ref[1:2, cs] + a_up * cw_ref[2:3, cs]
                + cb_ref[:, cs])
        f_s[:, cs] = (jax.nn.gelu(conv, approximate=True) * st[c]["gate"]).astype(BF16)

    _emit_staggered((stage_up, stage_act), n_chunks)
    n_sub = FFN_DOWN_SUBTILES if tm % (FFN_DOWN_SUBTILES * 16) == 0 else 1
    subs = [pl.ds(r * (tm // n_sub), tm // n_sub) for r in range(n_sub)]
    ys = [jnp.dot(f_s[rows, :], wd_ref[...], preferred_element_type=F32) for rows in subs]
    for rows, y in zip(subs, ys):
        out_ref[rows, :] = x1_ref[rows, :] + _rms(y, nw_ref[...])


def _ffn(x1, h2, S, wu, cw, cb, wd, nw):
    T = x1.shape[0]
    tm = min(FFN_ROW_TILE, S)
    assert S % tm == 0 and tm % HALO == 0 and D_FF % FF_CHUNK == 0
    nt = T // tm
    hb = tm // HALO
    row = lambda dt: pl.BlockSpec((tm, D_MODEL), lambda i: (i, 0))
    return pl.pallas_call(
        functools.partial(_ffn_kernel, tiles_per_seq=S // tm),
        grid=(nt,),
        in_specs=[row(F32), row(BF16),
                  pl.BlockSpec((HALO, D_MODEL), lambda i: (jnp.maximum(i * hb - 1, 0), 0)),
                  pl.BlockSpec((HALO, D_MODEL),
                               lambda i: (jnp.minimum((i + 1) * hb, nt * hb - 1), 0)),
                  _const_spec((D_MODEL, 2 * D_FF)), _const_spec((3, D_FF)),
                  _const_spec((1, D_FF)), _const_spec((D_FF, D_MODEL)),
                  _const_spec((1, D_MODEL))],
        out_specs=row(F32),
        out_shape=jax.ShapeDtypeStruct((T, D_MODEL), F32),
        scratch_shapes=[pltpu.VMEM((tm, D_FF), BF16)],
        compiler_params=pltpu.CompilerParams(dimension_semantics=("parallel",),
                                             vmem_limit_bytes=VMEM_LIMIT_BYTES),
        name="ffn",
    )(x1, h2, h2, h2, wu, cw, cb, wd, nw)


def _pack_tail(t):
    n_gate = IN_WIDTH - HEAD_WIDTH - 2 * D_MODEL
    pad = jnp.zeros(t.shape[:-1] + (GATE_SLAB - n_gate,), t.dtype)
    return jnp.concatenate([t[..., HEAD_WIDTH:HEAD_WIDTH + n_gate], pad,
                            t[..., HEAD_WIDTH + n_gate:]], axis=-1)


def _encoder_layer(x, p):
    B, S, _ = x.shape
    T = B * S
    x2d = x.reshape(T, D_MODEL)
    aq, ak, av, mqkv, mo, gl, gates = _inproj(x2d, p["pre_mix_norm"], p["w_in"], p["b_in"],
                                              p["w_in_tail"], p["b_in_tail"])
    ya = _attention(aq, ak, av, p["attn_sink"], B, S).reshape(T, D_MODEL)
    seq = lambda t: t.reshape(B, S, t.shape[-1])
    hf, hb = _mlstm(seq(mqkv), seq(gl))
    x1, h2 = _merge(x2d, ya, hf.reshape(T, D_MODEL), hb.reshape(T, D_MODEL), mo, gates,
                    p["mlstm_norm"], p["w_att_branch"], p["w_mlstm_branch"], p["w_out"],
                    p["post_mix_norm"], p["pre_ffn_norm"])
    y = _ffn(x1, h2, S, p["w_ffn_up"], p["ffn_conv_w"], p["ffn_conv_b"], p["w_ffn_down"],
             p["post_ffn_norm"])
    return y.reshape(B, S, D_MODEL)


def kernel(x_prompt, x_sample, pre_mix_norm, w_in, b_in, attn_sink, mlstm_norm, w_att_branch,
           w_mlstm_branch, w_out, post_mix_norm, pre_ffn_norm, w_ffn_up, ffn_conv_w, ffn_conv_b,
           w_ffn_down, post_ffn_norm):
    depth = w_in.shape[0]
    layers = []
    for l in range(depth):
        assert w_in.shape[-1] == IN_WIDTH
        w_bf = w_in[l].astype(BF16)
        layers.append(dict(
            pre_mix_norm=pre_mix_norm[l][None, :], w_in=w_bf, b_in=b_in[l][None, :],
            w_in_tail=_pack_tail(w_bf), b_in_tail=_pack_tail(b_in[l])[None, :],
            attn_sink=attn_sink[l], mlstm_norm=mlstm_norm[l][None, :],
            w_att_branch=w_att_branch[l].astype(BF16),
            w_mlstm_branch=w_mlstm_branch[l].astype(BF16), w_out=w_out[l].astype(BF16),
            post_mix_norm=post_mix_norm[l][None, :], pre_ffn_norm=pre_ffn_norm[l][None, :],
            w_ffn_up=w_ffn_up[l].astype(BF16), ffn_conv_w=ffn_conv_w[l],
            ffn_conv_b=ffn_conv_b[l][None, :], w_ffn_down=w_ffn_down[l].astype(BF16),
            post_ffn_norm=post_ffn_norm[l][None, :]))

    def trunk(x):
        for p in layers:
            x = _encoder_layer(x, p)
        return x

    return (trunk(x_prompt), trunk(x_sample))
```

```python
import functools

import jax
import jax.numpy as jnp
from jax import lax
from jax.experimental import pallas as pl
from jax.experimental.pallas import tpu as pltpu

F32 = jnp.float32
BF16 = jnp.bfloat16

D_MODEL = 1024
N_ATT_HEADS = 16
N_KV_GROUPS = 4
ATT_HEAD_DIM = 64
HEADS_PER_GROUP = N_ATT_HEADS // N_KV_GROUPS
KV_WIDTH = N_KV_GROUPS * ATT_HEAD_DIM
WINDOW = 128
BLOCK = 128
N_MLSTM_HEADS = 4
MLSTM_HEAD_DIM = 256
FORGET_BIAS = 3.0
D_FF = 2816
EPS = 1e-6
MASK_VALUE = -1e30
LOG2E = 1.4426950408889634

LANES = 128
GATE_SLAB = LANES
VMEM_LIMIT_BYTES = 56 * 1024 * 1024

_HEAD_SLABS = {"aq": (0, 1024), "ak": (1024, 256), "av": (1280, 256), "mq": (1536, 1024),
               "mk": (2560, 1024), "mv": (3584, 1024), "mo": (4608, 1024)}
HEAD_WIDTH = 5632
IN_WIDTH = HEAD_WIDTH + 16 + 2 * D_MODEL
_TAIL_SLABS = {"gl": (0, GATE_SLAB), "ga": (GATE_SLAB, 1024), "gm": (GATE_SLAB + 1024, 1024)}
TAIL_WIDTH = GATE_SLAB + 2 * D_MODEL

ROW_TILE = 512
FFN_ROW_TILE = 1024
MERGE_SUBTILES = 2
ATTN_BLOCKS_PER_STEP = 2
FFN_DOWN_SUBTILES = 2
FF_CHUNK = 256


def _const_spec(shape):
    nd = len(shape)
    return pl.BlockSpec(shape, lambda *_: (0,) * nd, pipeline_mode=pl.Buffered(1))


def _rms(x, w):
    return x * lax.rsqrt(jnp.mean(x * x, axis=-1, keepdims=True) + EPS) * w


def _emit_staggered(stages, n):
    for step in range(n + len(stages) - 1):
        for depth, stage in enumerate(stages):
            if 0 <= step - depth < n:
                stage(step - depth)


def _inproj_kernel(x_ref, nw_ref, w_ref, b_ref, wt_ref, bt_ref, aq_ref, ak_ref, av_ref, mqkv_ref,
                   mo_ref, gl_ref, gates_ref):
    x = x_ref[...]
    h = (x * nw_ref[...]).astype(BF16)
    r = lax.rsqrt(jnp.mean(x * x, axis=-1, keepdims=True) + EPS)
    D = D_MODEL

    def proj(name):
        if name in _HEAD_SLABS:
            (s, w), wr, br = _HEAD_SLABS[name], w_ref, b_ref
        else:
            (s, w), wr, br = _TAIL_SLABS[name], wt_ref, bt_ref
        return r * jnp.dot(h, wr[:, s:s + w], preferred_element_type=F32) + br[:, s:s + w]

    gates_ref[:, 0:D] = jax.nn.sigmoid(proj("ga")).astype(BF16)
    gates_ref[:, D:2 * D] = jax.nn.sigmoid(proj("gm")).astype(BF16)
    aq_ref[...] = (proj("aq") * (ATT_HEAD_DIM ** -0.5 * LOG2E)).astype(BF16)
    ak_ref[...] = proj("ak").astype(BF16)
    av_ref[...] = proj("av").astype(BF16)
    mqkv_ref[:, 0:D] = proj("mq").astype(BF16)
    mqkv_ref[:, D:2 * D] = (proj("mk") * (MLSTM_HEAD_DIM ** -0.5)).astype(BF16)
    mqkv_ref[:, 2 * D:3 * D] = proj("mv").astype(BF16)
    mo_ref[...] = proj("mo").astype(BF16)
    gl_ref[...] = proj("gl")


def _inproj(x2d, norm_w, w, b, w_tail, b_tail):
    T = x2d.shape[0]
    tm = min(ROW_TILE, T)
    assert T % tm == 0
    row = lambda w: pl.BlockSpec((tm, w), lambda i: (i, 0))
    outs = ((_HEAD_SLABS["aq"][1], BF16), (_HEAD_SLABS["ak"][1], BF16),
            (_HEAD_SLABS["av"][1], BF16), (3 * D_MODEL, BF16), (D_MODEL, BF16),
            (GATE_SLAB, F32), (2 * D_MODEL, BF16))
    return pl.pallas_call(
        _inproj_kernel,
        grid=(T // tm,),
        in_specs=[row(D_MODEL), _const_spec((1, D_MODEL)),
                  _const_spec((D_MODEL, IN_WIDTH)), _const_spec((1, IN_WIDTH)),
                  _const_spec((D_MODEL, TAIL_WIDTH)), _const_spec((1, TAIL_WIDTH))],
        out_specs=tuple(row(w) for w, _ in outs),
        out_shape=tuple(jax.ShapeDtypeStruct((T, w), dt) for w, dt in outs),
        compiler_params=pltpu.CompilerParams(dimension_semantics=("parallel",),
                                             vmem_limit_bytes=VMEM_LIMIT_BYTES),
        name="inproj",
    )(x2d, norm_w, w, b, w_tail, b_tail)


def _attn_kernel(sink_ref, q_ref, k_ref, v_ref, bias_ref, o_ref, *, nb):
    hd = ATT_HEAD_DIM
    nt = (((1,), (1,)), ((), ()))
    v_lane = lax.broadcasted_iota(jnp.int32, (3 * BLOCK, LANES), 1)
    work = []
    for sub in range(ATTN_BLOCKS_PER_STEP):
        j = pl.program_id(1) * ATTN_BLOCKS_PER_STEP + sub
        variant = jnp.where(j == 0, 0, jnp.where(j == nb - 1, 2, 1))
        kv_rows = [pl.ds(pl.multiple_of(blk * BLOCK, BLOCK), BLOCK)
                   for blk in (jnp.maximum(j - 1, 0), j, jnp.minimum(j + 1, nb - 1))]
        work.append((slice(sub * BLOCK, (sub + 1) * BLOCK), kv_rows, variant))
    items = [(w, g) for w in work for g in range(N_KV_GROUPS)]
    scores = [None] * len(items)

    def stage_scores(i):
        (q_rows, kv_rows, _), g = items[i]
        kg = jnp.concatenate([k_ref[0, r, g * hd:(g + 1) * hd] for r in kv_rows], axis=0)
        qs = jnp.concatenate([q_ref[0, q_rows, (g * HEADS_PER_GROUP + r) * hd:
                                    (g * HEADS_PER_GROUP + r + 1) * hd]
                              for r in range(HEADS_PER_GROUP)], axis=0)
        scores[i] = lax.dot_general(qs, kg, nt, preferred_element_type=F32)

    def stage_softmax_pv(i):
        (q_rows, kv_rows, variant), g = items[i]
        s = scores[i]
        pair = jnp.concatenate([v_ref[0, r, (g // 2) * LANES:(g // 2 + 1) * LANES]
                                for r in kv_rows], axis=0)
        if g % 2 == 0:
            vals, ones_at = slice(0, hd), hd
            vg = jnp.where(v_lane < hd, pair, (v_lane == ones_at).astype(BF16))
        else:
            vals, ones_at = slice(hd, 2 * hd), 0
            vg = jnp.where(v_lane >= hd, pair, (v_lane == ones_at).astype(BF16))
        ps, sink_terms = [], []
        for r in range(HEADS_PER_GROUP):
            h = g * HEADS_PER_GROUP + r
            sh = s[r * BLOCK:(r + 1) * BLOCK] + bias_ref[variant, h]
            sink = sink_ref[h] * LOG2E
            m = jnp.maximum(jnp.max(sh, axis=-1, keepdims=True), sink)
            ps.append(jnp.exp2(sh - m).astype(BF16))
            sink_terms.append(jnp.exp2(sink - m))
        o = jnp.dot(jnp.concatenate(ps, axis=0), vg, preferred_element_type=F32)
        outs = []
        for r in range(HEADS_PER_GROUP):
            o_r = o[r * BLOCK:(r + 1) * BLOCK]
            outs.append(o_r[:, vals] / (o_r[:, ones_at:ones_at + 1] + sink_terms[r]))
        gs = slice(g * HEADS_PER_GROUP * hd, (g + 1) * HEADS_PER_GROUP * hd)
        o_ref[0, q_rows, gs] = jnp.concatenate(outs, axis=1).astype(BF16)

    for i in range(len(items)):
        stage_scores(i)
    for i in range(len(items)):
        stage_softmax_pv(i)


def _attention_bias():
    rel = (jnp.arange(3 * BLOCK)[None, :] - BLOCK) - jnp.arange(BLOCK)[:, None]
    slopes = 2.0 ** (-8.0 * jnp.arange(1, N_ATT_HEADS + 1, dtype=F32) / N_ATT_HEADS)
    alibi = -slopes[:, None, None] * jnp.abs(rel).astype(F32)[None] * LOG2E
    band = (jnp.abs(rel) <= WINDOW)[None]
    col = jnp.arange(3 * BLOCK)[None, None, :]
    variants = []
    for has_prev, has_next in ((False, True), (True, True), (True, False)):
        ok = band & ((col >= BLOCK) | has_prev) & ((col < 2 * BLOCK) | has_next)
        variants.append(jnp.where(ok, alibi, MASK_VALUE))
    return jnp.stack(variants)


def _attention(aq, ak, av, sink, B, S):
    nb = S // BLOCK
    qrows = ATTN_BLOCKS_PER_STEP * BLOCK
    assert S % qrows == 0 and nb >= 2
    q3 = aq.reshape(B, S, N_ATT_HEADS * ATT_HEAD_DIM)
    k3 = ak.reshape(B, S, KV_WIDTH)
    v3 = av.reshape(B, S, KV_WIDTH)
    whole_seq = lambda w: pl.BlockSpec((1, S, w), lambda b, j: (b, 0, 0))
    return pl.pallas_call(
        functools.partial(_attn_kernel, nb=nb),
        grid=(B, S // qrows),
        in_specs=[pl.BlockSpec(memory_space=pltpu.SMEM),
                  pl.BlockSpec((1, qrows, D_MODEL), lambda b, j: (b, j, 0)),
                  whole_seq(KV_WIDTH), whole_seq(KV_WIDTH),
                  _const_spec((3, N_ATT_HEADS, BLOCK, 3 * BLOCK))],
        out_specs=pl.BlockSpec((1, qrows, D_MODEL), lambda b, j: (b, j, 0)),
        out_shape=jax.ShapeDtypeStruct((B, S, D_MODEL), BF16),
        compiler_params=pltpu.CompilerParams(dimension_semantics=("arbitrary", "arbitrary"),
                                             vmem_limit_bytes=VMEM_LIMIT_BYTES),
        name="attention",
    )(sink, q3, k3, v3, _attention_bias())


MLSTM_CHUNK = 256
AUG = MLSTM_HEAD_DIM + LANES


def _lanes(x, tiles):
    return jnp.concatenate([x] * tiles, axis=1)


def _gate_vectors(g_ref, rows, reverse):
    L, nh = MLSTM_CHUNK, N_MLSTM_HEADS
    gl_raw = g_ref[0, rows, :]
    gl = gl_raw * LOG2E
    lf_all = jax.nn.log_sigmoid(gl_raw + FORGET_BIAS) * LOG2E
    t_idx = lax.broadcasted_iota(jnp.int32, (L, L), 0)
    s_idx = lax.broadcasted_iota(jnp.int32, (L, L), 1)
    seen = (s_idx >= t_idx) if reverse else (s_idx <= t_idx)
    unseen_bias = jnp.where(seen, 0.0, -jnp.inf)
    seen_bf = seen.astype(BF16)
    lf_hi = lf_all.astype(BF16)
    lf_r = lf_all - lf_hi.astype(F32)
    lf_mid = lf_r.astype(BF16)
    lf_lo = (lf_r - lf_mid.astype(F32)).astype(BF16)
    bcol_all = (jnp.dot(seen_bf, lf_hi, preferred_element_type=F32)
                + jnp.dot(seen_bf, lf_mid, preferred_element_type=F32)
                + jnp.dot(seen_bf, lf_lo, preferred_element_type=F32))
    brow_all = bcol_all.T
    lirow_all = gl.T
    direction = 1 if reverse else 0
    last = 0 if reverse else L - 1
    per_head = []
    for h in range(nh):
        ci = direction * nh + h
        cf = 2 * nh + ci
        b_rep = jnp.broadcast_to(bcol_all[:, cf:cf + 1], (L, LANES))
        per_head.append(dict(b=b_rep,
                             li=jnp.broadcast_to(gl[:, ci:ci + 1], (L, LANES)),
                             u_row=lirow_all[ci:ci + 1, :] - brow_all[cf:cf + 1, :],
                             b_last=b_rep[last:last + 1, :]))
    return unseen_bias, per_head


def _mlstm_kernel(xf_ref, gf_ref, xb_ref, gb_ref, hf_ref, hb_ref, c_s, m_s):
    @pl.when(pl.program_id(1) == 0)
    def _():
        c_s[...] = jnp.zeros_like(c_s)
        m_s[...] = jnp.zeros_like(m_s)

    rows = slice(None)
    _mlstm_chunk_pair((xf_ref, gf_ref, hf_ref, rows), (xb_ref, gb_ref, hb_ref, rows), c_s, m_s)


def _mlstm_chunk_pair(fwd, bwd, c_s, m_s):
    L, dh, nh = MLSTM_CHUNK, MLSTM_HEAD_DIM, N_MLSTM_HEADS
    W = nh * dh
    n_aug = AUG // LANES
    ones_block = jnp.ones((L, LANES), BF16)
    nt = (((1,), (1,)), ((), ()))
    tn = (((0,), (0,)), ((), ()))

    chains = []
    for direction, (x_ref, _, o_ref, rows) in enumerate((fwd, bwd)):
        for h in range(nh):
            cols = [slice(part * W + h * dh, part * W + (h + 1) * dh) for part in range(3)]
            chains.append((x_ref, o_ref, rows, cols, cols[0], direction * nh + h))
    n = len(chains)
    st = [dict() for _ in range(n)]

    mask_f, gates_f = _gate_vectors(fwd[1], fwd[3], reverse=False)
    mask_b, gates_b = _gate_vectors(bwd[1], bwd[3], reverse=True)
    gates = gates_f + gates_b
    masks = [mask_f] * nh + [mask_b] * nh

    for i, (x_ref, _, rows, (qs, ks, vs), _, slot) in enumerate(chains):
        s = st[i]
        s["c_prev"] = c_s[slot]
        s["m_prev"] = m_s[slot]
        s["v_aug"] = jnp.concatenate([x_ref[0, rows, vs], ones_block], axis=1)
        s["qk"] = lax.dot_general(x_ref[0, rows, qs], x_ref[0, rows, ks], nt,
                                  preferred_element_type=F32)
        s["inter"] = jnp.dot(x_ref[0, rows, qs], s["c_prev"].astype(BF16),
                             preferred_element_type=F32)

    for i, (x_ref, _, rows, (_, ks, _), _, slot) in enumerate(chains):
        s, gv = st[i], gates[i]
        g_rep = gv["b_last"] - gv["b"] + gv["li"]
        m_new = jnp.maximum(gv["b_last"] + s["m_prev"], jnp.max(g_rep, axis=0, keepdims=True))
        decay = jnp.exp2(gv["b_last"] + s["m_prev"] - m_new)
        wk = (_lanes(jnp.exp2(g_rep - m_new), dh // LANES)
              * x_ref[0, rows, ks].astype(F32)).astype(BF16)
        c_s[slot] = _lanes(decay, n_aug) * s["c_prev"] + lax.dot_general(
            wk, s["v_aug"], tn, preferred_element_type=F32)
        m_s[slot] = m_new

    for i in range(n):
        s, gv = st[i], gates[i]
        d = (_lanes(gv["b"], L // LANES) + gv["u_row"]) + masks[i]
        s["m_t"] = jnp.maximum(gv["b"] + s["m_prev"], jnp.max(d, axis=-1, keepdims=True))
        a = (s["qk"] * jnp.exp2(d - _lanes(s["m_t"], L // LANES))).astype(BF16)
        s["intra"] = jnp.dot(a, s["v_aug"], preferred_element_type=F32)

    for i, (_, o_ref, rows, _, hs, _) in enumerate(chains):
        s, gv = st[i], gates[i]
        w_inter = jnp.exp2(gv["b"] + s["m_prev"] - s["m_t"])
        numden = _lanes(w_inter, n_aug) * s["inter"] + s["intra"]
        den = numden[:, dh:]
        scale = 1.0 / jnp.maximum(jnp.abs(den), jnp.exp2(-s["m_t"]))
        o_ref[0, rows, hs] = (numden[:, :dh] * _lanes(scale, dh // LANES)).astype(o_ref.dtype)


def _mlstm(qkv3, g3):
    B, S, W3 = qkv3.shape
    W = W3 // 3
    L = MLSTM_CHUNK
    nc = S // L
    assert S % L == 0 and W == N_MLSTM_HEADS * MLSTM_HEAD_DIM
    fwd = lambda w: pl.BlockSpec((1, L, w), lambda b, c: (b, c, 0))
    bwd = lambda w: pl.BlockSpec((1, L, w), lambda b, c: (b, nc - 1 - c, 0))
    n_chains = 2 * N_MLSTM_HEADS
    return pl.pallas_call(
        _mlstm_kernel,
        grid=(B, nc),
        in_specs=[fwd(W3), fwd(GATE_SLAB), bwd(W3), bwd(GATE_SLAB)],
        out_specs=(fwd(W), bwd(W)),
        out_shape=(jax.ShapeDtypeStruct((B, S, W), BF16), jax.ShapeDtypeStruct((B, S, W), BF16)),
        scratch_shapes=[pltpu.VMEM((n_chains, MLSTM_HEAD_DIM, AUG), F32),
                        pltpu.VMEM((n_chains, 1, LANES), F32)],
        compiler_params=pltpu.CompilerParams(dimension_semantics=("arbitrary", "arbitrary"),
                                             vmem_limit_bytes=VMEM_LIMIT_BYTES),
        name="mlstm",
    )(qkv3, g3, qkv3, g3)


def _merge_kernel(x_ref, ya_ref, hf_ref, hb_ref, mo_ref, gates_ref, mnw_ref, wa_ref, wm_ref,
                  wo_ref, pmw_ref, pfw_ref, x1_ref, h2_ref):
    dh = MLSTM_HEAD_DIM
    tm = x_ref.shape[0]
    n_sub = MERGE_SUBTILES if tm % (MERGE_SUBTILES * 16) == 0 else 1
    subs = [pl.ds(r * (tm // n_sub), tm // n_sub) for r in range(n_sub)]
    st = [dict() for _ in subs]

    def stage_recurrent(i):
        rows, rec = subs[i], None
        for h in range(N_MLSTM_HEADS):
            hs = slice(h * dh, (h + 1) * dh)
            o = (jax.nn.sigmoid(mo_ref[rows, hs].astype(F32))
                 * (hf_ref[rows, hs].astype(F32) + hb_ref[rows, hs].astype(F32)))
            part = jnp.dot(_rms(o, mnw_ref[:, hs]).astype(BF16), wm_ref[hs, :],
                           preferred_element_type=F32)
            rec = part if rec is None else rec + part
        st[i]["rec"] = rec

    def stage_attention(i):
        rows = subs[i]
        st[i]["att"] = gates_ref[rows, :D_MODEL].astype(F32) * jnp.dot(
            ya_ref[rows, :], wa_ref[...], preferred_element_type=F32)

    def stage_mix(i):
        rows = subs[i]
        merged = st[i]["att"] + gates_ref[rows, D_MODEL:].astype(F32) * st[i]["rec"]
        st[i]["mixed"] = jnp.dot(merged.astype(BF16), wo_ref[...], preferred_element_type=F32)

    def stage_norms(i):
        rows = subs[i]
        x1 = x_ref[rows, :] + _rms(st[i]["mixed"], pmw_ref[...])
        x1_ref[rows, :] = x1
        h2_ref[rows, :] = _rms(x1, pfw_ref[...]).astype(BF16)

    _emit_staggered((stage_recurrent, stage_attention, stage_mix, stage_norms), n_sub)


def _merge(x2d, ya, hf, hb, mo, gates, mnw, wa, wm, wo, pmw, pfw):
    T = x2d.shape[0]
    tm = min(ROW_TILE, T)
    assert T % tm == 0
    row = pl.BlockSpec((tm, D_MODEL), lambda i: (i, 0))
    sq = _const_spec((D_MODEL, D_MODEL))
    vec = _const_spec((1, D_MODEL))
    return pl.pallas_call(
        _merge_kernel,
        grid=(T // tm,),
        in_specs=[row, row, row, row, row, pl.BlockSpec((tm, 2 * D_MODEL), lambda i: (i, 0)),
                  vec, sq, sq, sq, vec, vec],
        out_specs=(row, row),
        out_shape=(jax.ShapeDtypeStruct((T, D_MODEL), F32),
                   jax.ShapeDtypeStruct((T, D_MODEL), BF16)),
        compiler_params=pltpu.CompilerParams(dimension_semantics=("parallel",),
                                             vmem_limit_bytes=VMEM_LIMIT_BYTES),
        name="merge",
    )(x2d, ya, hf, hb, mo, gates, mnw, wa, wm, wo, pmw, pfw)


HALO = 16


def _ffn_kernel(x1_ref, h2_ref, hp_ref, hn_ref, wu_ref, cw_ref, cb_ref, wd_ref, nw_ref,
                out_ref, f_s, *, tiles_per_seq):
    i = pl.program_id(0)
    tm = h2_ref.shape[0]
    has_prev = (i % tiles_per_seq) != 0
    has_next = (i % tiles_per_seq) != tiles_per_seq - 1
    h2 = h2_ref[...]
    h2_ext = jnp.concatenate([hp_ref[...], h2, hn_ref[...]], axis=0)
    row = lax.broadcasted_iota(jnp.int32, (tm, 1), 0)
    n_chunks = D_FF // FF_CHUNK
    st = [dict() for _ in range(n_chunks)]

    def stage_up(c):
        cs = slice(c * FF_CHUNK, (c + 1) * FF_CHUNK)
        gs = slice(D_FF + c * FF_CHUNK, D_FF + (c + 1) * FF_CHUNK)
        st[c]["a_ext"] = jnp.dot(h2_ext, wu_ref[:, cs], preferred_element_type=F32)
        st[c]["gate"] = jnp.dot(h2, wu_ref[:, gs], preferred_element_type=F32)

    def stage_act(c):
        cs = slice(c * FF_CHUNK, (c + 1) * FF_CHUNK)
        a_ext = st[c]["a_ext"]
        a = a_ext[HALO:HALO + tm]
        a_prev = jnp.where(has_prev, a_ext[HALO - 1:HALO], 0.0)
        a_next = jnp.where(has_next, a_ext[HALO + tm:HALO + tm + 1], 0.0)
        a_dn = jnp.where(row == 0, a_prev, pltpu.roll(a, 1, axis=0))
        a_up = jnp.where(row == tm - 1, a_next, pltpu.roll(a, tm - 1, axis=0))
        conv = (a_dn * cw_ref[0:1, cs] + a * cw_ref[1:2, cs] + a_up * cw_ref[2:3, cs]
                + cb_ref[:, cs])
        f_s[:, cs] = (jax.nn.gelu(conv, approximate=True) * st[c]["gate"]).astype(BF16)

    _emit_staggered((stage_up, stage_act), n_chunks)
    n_sub = FFN_DOWN_SUBTILES if tm % (FFN_DOWN_SUBTILES * 16) == 0 else 1
    subs = [pl.ds(r * (tm // n_sub), tm // n_sub) for r in range(n_sub)]
    ys = [jnp.dot(f_s[rows, :], wd_ref[...], preferred_element_type=F32) for rows in subs]
    for rows, y in zip(subs, ys):
        out_ref[rows, :] = x1_ref[rows, :] + _rms(y, nw_ref[...])


def _ffn(x1, h2, S, wu, cw, cb, wd, nw):
    T = x1.shape[0]
    tm = min(FFN_ROW_TILE, S)
    assert S % tm == 0 and tm % HALO == 0 and D_FF % FF_CHUNK == 0
    nt = T // tm
    hb = tm // HALO
    row = lambda dt: pl.BlockSpec((tm, D_MODEL), lambda i: (i, 0))
    return pl.pallas_call(
        functools.partial(_ffn_kernel, tiles_per_seq=S // tm),
        grid=(nt,),
        in_specs=[row(F32), row(BF16),
                  pl.BlockSpec((HALO, D_MODEL), lambda i: (jnp.maximum(i * hb - 1, 0), 0)),
                  pl.BlockSpec((HALO, D_MODEL),
                               lambda i: (jnp.minimum((i + 1) * hb, nt * hb - 1), 0)),
                  _const_spec((D_MODEL, 2 * D_FF)), _const_spec((3, D_FF)),
                  _const_spec((1, D_FF)), _const_spec((D_FF, D_MODEL)),
                  _const_spec((1, D_MODEL))],
        out_specs=row(F32),
        out_shape=jax.ShapeDtypeStruct((T, D_MODEL), F32),
        scratch_shapes=[pltpu.VMEM((tm, D_FF), BF16)],
        compiler_params=pltpu.CompilerParams(dimension_semantics=("parallel",),
                                             vmem_limit_bytes=VMEM_LIMIT_BYTES),
        name="ffn",
    )(x1, h2, h2, h2, wu, cw, cb, wd, nw)


def _pack_tail(t):
    n_gate = IN_WIDTH - HEAD_WIDTH - 2 * D_MODEL
    pad = jnp.zeros(t.shape[:-1] + (GATE_SLAB - n_gate,), t.dtype)
    return jnp.concatenate([t[..., HEAD_WIDTH:HEAD_WIDTH + n_gate], pad,
                            t[..., HEAD_WIDTH + n_gate:]], axis=-1)


def _encoder_layer(x, p):
    B, S, _ = x.shape
    T = B * S
    x2d = x.reshape(T, D_MODEL)
    aq, ak, av, mqkv, mo, gl, gates = _inproj(x2d, p["pre_mix_norm"], p["w_in"], p["b_in"],
                                              p["w_in_tail"], p["b_in_tail"])
    ya = _attention(aq, ak, av, p["attn_sink"], B, S).reshape(T, D_MODEL)
    seq = lambda t: t.reshape(B, S, t.shape[-1])
    hf, hb = _mlstm(seq(mqkv), seq(gl))
    x1, h2 = _merge(x2d, ya, hf.reshape(T, D_MODEL), hb.reshape(T, D_MODEL), mo, gates,
                    p["mlstm_norm"], p["w_att_branch"], p["w_mlstm_branch"], p["w_out"],
                    p["post_mix_norm"], p["pre_ffn_norm"])
    y = _ffn(x1, h2, S, p["w_ffn_up"], p["ffn_conv_w"], p["ffn_conv_b"], p["w_ffn_down"],
             p["post_ffn_norm"])
    return y.reshape(B, S, D_MODEL)


def kernel(x_prompt, x_sample, pre_mix_norm, w_in, b_in, attn_sink, mlstm_norm, w_att_branch,
           w_mlstm_branch, w_out, post_mix_norm, pre_ffn_norm, w_ffn_up, ffn_conv_w, ffn_conv_b,
           w_ffn_down, post_ffn_norm):
    depth = w_in.shape[0]
    layers = []
    for l in range(depth):
        assert w_in.shape[-1] == IN_WIDTH
        w_bf = w_in[l].astype(BF16)
        layers.append(dict(
            pre_mix_norm=pre_mix_norm[l][None, :], w_in=w_bf, b_in=b_in[l][None, :],
            w_in_tail=_pack_tail(w_bf), b_in_tail=_pack_tail(b_in[l])[None, :],
            attn_sink=attn_sink[l], mlstm_norm=mlstm_norm[l][None, :],
            w_att_branch=w_att_branch[l].astype(BF16),
            w_mlstm_branch=w_mlstm_branch[l].astype(BF16), w_out=w_out[l].astype(BF16),
            post_mix_norm=post_mix_norm[l][None, :], pre_ffn_norm=pre_ffn_norm[l][None, :],
            w_ffn_up=w_ffn_up[l].astype(BF16), ffn_conv_w=ffn_conv_w[l],
            ffn_conv_b=ffn_conv_b[l][None, :], w_ffn_down=w_ffn_down[l].astype(BF16),
            post_ffn_norm=post_ffn_norm[l][None, :]))

    def trunk(x):
        for p in layers:
            x = _encoder_layer(x, p)
        return x

    return (trunk(x_prompt), trunk(x_sample))
```

```python
import functools

import jax
import jax.numpy as jnp
from jax import lax
from jax.experimental import pallas as pl
from jax.experimental.pallas import tpu as pltpu

F32 = jnp.float32
BF16 = jnp.bfloat16

D_MODEL = 1024
N_ATT_HEADS = 16
N_KV_GROUPS = 4
ATT_HEAD_DIM = 64
HEADS_PER_GROUP = N_ATT_HEADS // N_KV_GROUPS
KV_WIDTH = N_KV_GROUPS * ATT_HEAD_DIM
WINDOW = 128
BLOCK = 128
N_MLSTM_HEADS = 4
MLSTM_HEAD_DIM = 256
FORGET_BIAS = 3.0
D_FF = 2816
EPS = 1e-6
MASK_VALUE = -1e30
LOG2E = 1.4426950408889634

LANES = 128
GATE_SLAB = LANES
VMEM_LIMIT_BYTES = 56 * 1024 * 1024

_HEAD_SLABS = {"aq": (0, 1024), "ak": (1024, 256), "av": (1280, 256), "mq": (1536, 1024),
               "mk": (2560, 1024), "mv": (3584, 1024), "mo": (4608, 1024)}
HEAD_WIDTH = 5632
IN_WIDTH = HEAD_WIDTH + 16 + 2 * D_MODEL
_TAIL_SLABS = {"gl": (0, GATE_SLAB), "ga": (GATE_SLAB, 1024), "gm": (GATE_SLAB + 1024, 1024)}
TAIL_WIDTH = GATE_SLAB + 2 * D_MODEL

ROW_TILE = 512
FFN_ROW_TILE = 1024
MERGE_SUBTILES = 2
ATTN_BLOCKS_PER_STEP = 2
FFN_DOWN_SUBTILES = 2
FF_CHUNK = 256


def _const_spec(shape):
    nd = len(shape)
    return pl.BlockSpec(shape, lambda *_: (0,) * nd, pipeline_mode=pl.Buffered(1))


def _rms(x, w):
    return x * lax.rsqrt(jnp.mean(x * x, axis=-1, keepdims=True) + EPS) * w


def _emit_staggered(stages, n):
    for step in range(n + len(stages) - 1):
        for depth, stage in enumerate(stages):
            if 0 <= step - depth < n:
                stage(step - depth)


def _inproj_kernel(x_ref, nw_ref, w_ref, b_ref, wt_ref, bt_ref, aq_ref, ak_ref, av_ref, mqkv_ref,
                   mo_ref, gl_ref, gates_ref):
    x = x_ref[...]
    h = (x * nw_ref[...]).astype(BF16)
    r = lax.rsqrt(jnp.mean(x * x, axis=-1, keepdims=True) + EPS)
    D = D_MODEL

    def proj(name):
        if name in _HEAD_SLABS:
            (s, w), wr, br = _HEAD_SLABS[name], w_ref, b_ref
        else:
            (s, w), wr, br = _TAIL_SLABS[name], wt_ref, bt_ref
        return r * jnp.dot(h, wr[:, s:s + w], preferred_element_type=F32) + br[:, s:s + w]

    gates_ref[:, 0:D] = jax.nn.sigmoid(proj("ga")).astype(BF16)
    gates_ref[:, D:2 * D] = jax.nn.sigmoid(proj("gm")).astype(BF16)
    aq_ref[...] = (proj("aq") * (ATT_HEAD_DIM ** -0.5 * LOG2E)).astype(BF16)
    ak_ref[...] = proj("ak").astype(BF16)
    av_ref[...] = proj("av").astype(BF16)
    mqkv_ref[:, 0:D] = proj("mq").astype(BF16)
    mqkv_ref[:, D:2 * D] = (proj("mk") * (MLSTM_HEAD_DIM ** -0.5)).astype(BF16)
    mqkv_ref[:, 2 * D:3 * D] = proj("mv").astype(BF16)
    mo_ref[...] = proj("mo").astype(BF16)
    gl_ref[...] = proj("gl")


def _inproj(x2d, norm_w, w, b, w_tail, b_tail):
    T = x2d.shape[0]
    tm = min(ROW_TILE, T)
    assert T % tm == 0
    row = lambda w: pl.BlockSpec((tm, w), lambda i: (i, 0))
    outs = ((_HEAD_SLABS["aq"][1], BF16), (_HEAD_SLABS["ak"][1], BF16),
            (_HEAD_SLABS["av"][1], BF16), (3 * D_MODEL, BF16), (D_MODEL, BF16),
            (GATE_SLAB, F32), (2 * D_MODEL, BF16))
    return pl.pallas_call(
        _inproj_kernel,
        grid=(T // tm,),
        in_specs=[row(D_MODEL), _const_spec((1, D_MODEL)),
                  _const_spec((D_MODEL, IN_WIDTH)), _const_spec((1, IN_WIDTH)),
                  _const_spec((D_MODEL, TAIL_WIDTH)), _const_spec((1, TAIL_WIDTH))],
        out_specs=tuple(row(w) for w, _ in outs),
        out_shape=tuple(jax.ShapeDtypeStruct((T, w), dt) for w, dt in outs),
        compiler_params=pltpu.CompilerParams(dimension_semantics=("parallel",),
                                             vmem_limit_bytes=VMEM_LIMIT_BYTES),
        name="inproj",
    )(x2d, norm_w, w, b, w_tail, b_tail)


def _attn_kernel(sink_ref, q_ref, k_ref, v_ref, bias_ref, o_ref, *, nb):
    hd = ATT_HEAD_DIM
    nt = (((1,), (1,)), ((), ()))
    v_lane = lax.broadcasted_iota(jnp.int32, (3 * BLOCK, LANES), 1)
    work = []
    for sub in range(ATTN_BLOCKS_PER_STEP):
        j = pl.program_id(1) * ATTN_BLOCKS_PER_STEP + sub
        variant = jnp.where(j == 0, 0, jnp.where(j == nb - 1, 2, 1))
        kv_rows = [pl.ds(pl.multiple_of(blk * BLOCK, BLOCK), BLOCK)
                   for blk in (jnp.maximum(j - 1, 0), j, jnp.minimum(j + 1, nb - 1))]
        work.append((slice(sub * BLOCK, (sub + 1) * BLOCK), kv_rows, variant))
    items = [(w, g) for w in work for g in range(N_KV_GROUPS)]
    scores = [None] * len(items)

    def stage_scores(i):
        (q_rows, kv_rows, _), g = items[i]
        kg = jnp.concatenate([k_ref[0, r, g * hd:(g + 1) * hd] for r in kv_rows], axis=0)
        qs = jnp.concatenate([q_ref[0, q_rows, (g * HEADS_PER_GROUP + r) * hd:
                                    (g * HEADS_PER_GROUP + r + 1) * hd]
                              for r in range(HEADS_PER_GROUP)], axis=0)
        scores[i] = lax.dot_general(qs, kg, nt, preferred_element_type=F32)

    def stage_softmax_pv(i):
        (q_rows, kv_rows, variant), g = items[i]
        s = scores[i]
        pair = jnp.concatenate([v_ref[0, r, (g // 2) * LANES:(g // 2 + 1) * LANES]
                                for r in kv_rows], axis=0)
        if g % 2 == 0:
            vals, ones_at = slice(0, hd), hd
            vg = jnp.where(v_lane < hd, pair, (v_lane == ones_at).astype(BF16))
        else:
            vals, ones_at = slice(hd, 2 * hd), 0
            vg = jnp.where(v_lane >= hd, pair, (v_lane == ones_at).astype(BF16))
        ps, sink_terms = [], []
        for r in range(HEADS_PER_GROUP):
            h = g * HEADS_PER_GROUP + r
            sh = s[r * BLOCK:(r + 1) * BLOCK] + bias_ref[variant, h]
            sink = sink_ref[h] * LOG2E
            m = jnp.maximum(jnp.max(sh, axis=-1, keepdims=True), sink)
            ps.append(jnp.exp2(sh - m).astype(BF16))
            sink_terms.append(jnp.exp2(sink - m))
        o = jnp.dot(jnp.concatenate(ps, axis=0), vg, preferred_element_type=F32)
        outs = []
        for r in range(HEADS_PER_GROUP):
            o_r = o[r * BLOCK:(r + 1) * BLOCK]
            outs.append(o_r[:, vals] / (o_r[:, ones_at:ones_at + 1] + sink_terms[r]))
        gs = slice(g * HEADS_PER_GROUP * hd, (g + 1) * HEADS_PER_GROUP * hd)
        o_ref[0, q_rows, gs] = jnp.concatenate(outs, axis=1).astype(BF16)

    for i in range(len(items)):
        stage_scores(i)
    for i in range(len(items)):
        stage_softmax_pv(i)


def _attention_bias():
    rel = (jnp.arange(3 * BLOCK)[None, :] - BLOCK) - jnp.arange(BLOCK)[:, None]
    slopes = 2.0 ** (-8.0 * jnp.arange(1, N_ATT_HEADS + 1, dtype=F32) / N_ATT_HEADS)
    alibi = -slopes[:, None, None] * jnp.abs(rel).astype(F32)[None] * LOG2E
    band = (jnp.abs(rel) <= WINDOW)[None]
    col = jnp.arange(3 * BLOCK)[None, None, :]
    variants = []
    for has_prev, has_next in ((False, True), (True, True), (True, False)):
        ok = band & ((col >= BLOCK) | has_prev) & ((col < 2 * BLOCK) | has_next)
        variants.append(jnp.where(ok, alibi, MASK_VALUE))
    return jnp.stack(variants)


def _attention(aq, ak, av, sink, B, S):
    nb = S // BLOCK
    qrows = ATTN_BLOCKS_PER_STEP * BLOCK
    assert S % qrows == 0 and nb >= 2
    q3 = aq.reshape(B, S, N_ATT_HEADS * ATT_HEAD_DIM)
    k3 = ak.reshape(B, S, KV_WIDTH)
    v3 = av.reshape(B, S, KV_WIDTH)
    whole_seq = lambda w: pl.BlockSpec((1, S, w), lambda b, j: (b, 0, 0))
    return pl.pallas_call(
        functools.partial(_attn_kernel, nb=nb),
        grid=(B, S // qrows),
        in_specs=[pl.BlockSpec(memory_space=pltpu.SMEM),
                  pl.BlockSpec((1, qrows, D_MODEL), lambda b, j: (b, j, 0)),
                  whole_seq(KV_WIDTH), whole_seq(KV_WIDTH),
                  _const_spec((3, N_ATT_HEADS, BLOCK, 3 * BLOCK))],
        out_specs=pl.BlockSpec((1, qrows, D_MODEL), lambda b, j: (b, j, 0)),
        out_shape=jax.ShapeDtypeStruct((B, S, D_MODEL), BF16),
        compiler_params=pltpu.CompilerParams(dimension_semantics=("arbitrary", "arbitrary"),
                                             vmem_limit_bytes=VMEM_LIMIT_BYTES),
        name="attention",
    )(sink, q3, k3, v3, _attention_bias())


MLSTM_CHUNK = 256
MLSTM_SEQS_PER_STEP = 2
AUG = MLSTM_HEAD_DIM + LANES


def _lanes(x, tiles):
    return jnp.concatenate([x] * tiles, axis=1)


def _gate_vectors(g_ref, rows, reverse):
    L, nh = MLSTM_CHUNK, N_MLSTM_HEADS
    gl_raw = g_ref[rows, :]
    gl = gl_raw * LOG2E
    lf_all = jax.nn.log_sigmoid(gl_raw + FORGET_BIAS) * LOG2E
    t_idx = lax.broadcasted_iota(jnp.int32, (L, L), 0)
    s_idx = lax.broadcasted_iota(jnp.int32, (L, L), 1)
    seen = (s_idx >= t_idx) if reverse else (s_idx <= t_idx)
    unseen_bias = jnp.where(seen, 0.0, -jnp.inf)
    seen_bf = seen.astype(BF16)
    lf_hi = lf_all.astype(BF16)
    lf_r = lf_all - lf_hi.astype(F32)
    lf_mid = lf_r.astype(BF16)
    lf_lo = (lf_r - lf_mid.astype(F32)).astype(BF16)
    bcol_all = (jnp.dot(seen_bf, lf_hi, preferred_element_type=F32)
                + jnp.dot(seen_bf, lf_mid, preferred_element_type=F32)
                + jnp.dot(seen_bf, lf_lo, preferred_element_type=F32))
    brow_all = bcol_all.T
    lirow_all = gl.T
    direction = 1 if reverse else 0
    last = 0 if reverse else L - 1
    per_head = []
    for h in range(nh):
        ci = direction * nh + h
        cf = 2 * nh + ci
        b_rep = jnp.broadcast_to(bcol_all[:, cf:cf + 1], (L, LANES))
        per_head.append(dict(b=b_rep,
                             li=jnp.broadcast_to(gl[:, ci:ci + 1], (L, LANES)),
                             u_row=lirow_all[ci:ci + 1, :] - brow_all[cf:cf + 1, :],
                             b_last=b_rep[last:last + 1, :]))
    return unseen_bias, per_head


def _mlstm_kernel(xf_ref, gf_ref, xb_ref, gb_ref, hf_ref, hb_ref, c_s, m_s):
    @pl.when(pl.program_id(1) == 0)
    def _():
        c_s[...] = jnp.zeros_like(c_s)
        m_s[...] = jnp.zeros_like(m_s)

    rows = slice(None)
    _mlstm_chunks([((xf_ref.at[bi], gf_ref.at[bi], hf_ref.at[bi], rows),
                    (xb_ref.at[bi], gb_ref.at[bi], hb_ref.at[bi], rows))
                   for bi in range(xf_ref.shape[0])], c_s, m_s)


def _mlstm_chunks(pairs, c_s, m_s):
    L, dh, nh = MLSTM_CHUNK, MLSTM_HEAD_DIM, N_MLSTM_HEADS
    W = nh * dh
    n_aug = AUG // LANES
    ones_block = jnp.ones((L, LANES), BF16)
    nt = (((1,), (1,)), ((), ()))
    tn = (((0,), (0,)), ((), ()))

    chains, gates, masks = [], [], []
    for bi, (fwd, bwd) in enumerate(pairs):
        for direction, (x_ref, g_ref, o_ref, rows) in enumerate((fwd, bwd)):
            for h in range(nh):
                cols = [slice(part * W + h * dh, part * W + (h + 1) * dh) for part in range(3)]
                chains.append((x_ref, o_ref, rows, cols, cols[0],
                               (bi * 2 + direction) * nh + h))
            mask, per_head = _gate_vectors(g_ref, rows, reverse=direction == 1)
            gates += per_head
            masks += [mask] * nh
    n = len(chains)
    st = [dict() for _ in range(n)]

    for i, (x_ref, _, rows, (qs, ks, vs), _, slot) in enumerate(chains):
        s = st[i]
        s["c_prev"] = c_s[slot]
        s["m_prev"] = m_s[slot]
        s["v_aug"] = jnp.concatenate([x_ref[rows, vs], ones_block], axis=1)
        s["qk"] = lax.dot_general(x_ref[rows, qs], x_ref[rows, ks], nt,
                                  preferred_element_type=F32)
        s["inter"] = jnp.dot(x_ref[rows, qs], s["c_prev"].astype(BF16),
                             preferred_element_type=F32)

    for i, (x_ref, _, rows, (_, ks, _), _, slot) in enumerate(chains):
        s, gv = st[i], gates[i]
        g_rep = gv["b_last"] - gv["b"] + gv["li"]
        m_new = jnp.maximum(gv["b_last"] + s["m_prev"], jnp.max(g_rep, axis=0, keepdims=True))
        decay = jnp.exp2(gv["b_last"] + s["m_prev"] - m_new)
        wk = (_lanes(jnp.exp2(g_rep - m_new), dh // LANES)
              * x_ref[rows, ks].astype(F32)).astype(BF16)
        c_s[slot] = _lanes(decay, n_aug) * s["c_prev"] + lax.dot_general(
            wk, s["v_aug"], tn, preferred_element_type=F32)
        m_s[slot] = m_new

    for i in range(n):
        s, gv = st[i], gates[i]
        d = (_lanes(gv["b"], L // LANES) + gv["u_row"]) + masks[i]
        s["m_t"] = jnp.maximum(gv["b"] + s["m_prev"], jnp.max(d, axis=-1, keepdims=True))
        a = (s["qk"] * jnp.exp2(d - _lanes(s["m_t"], L // LANES))).astype(BF16)
        s["intra"] = jnp.dot(a, s["v_aug"], preferred_element_type=F32)

    for i, (_, o_ref, rows, _, hs, _) in enumerate(chains):
        s, gv = st[i], gates[i]
        w_inter = jnp.exp2(gv["b"] + s["m_prev"] - s["m_t"])
        numden = _lanes(w_inter, n_aug) * s["inter"] + s["intra"]
        den = numden[:, dh:]
        scale = 1.0 / jnp.maximum(jnp.abs(den), jnp.exp2(-s["m_t"]))
        o_ref[rows, hs] = (numden[:, :dh] * _lanes(scale, dh // LANES)).astype(o_ref.dtype)


def _mlstm(qkv3, g3):
    B, S, W3 = qkv3.shape
    W = W3 // 3
    L = MLSTM_CHUNK
    nc = S // L
    assert S % L == 0 and W == N_MLSTM_HEADS * MLSTM_HEAD_DIM
    nseq = MLSTM_SEQS_PER_STEP if B % MLSTM_SEQS_PER_STEP == 0 else 1
    fwd = lambda w: pl.BlockSpec((nseq, L, w), lambda b, c: (b, c, 0))
    bwd = lambda w: pl.BlockSpec((nseq, L, w), lambda b, c: (b, nc - 1 - c, 0))
    n_chains = nseq * 2 * N_MLSTM_HEADS
    return pl.pallas_call(
        _mlstm_kernel,
        grid=(B // nseq, nc),
        in_specs=[fwd(W3), fwd(GATE_SLAB), bwd(W3), bwd(GATE_SLAB)],
        out_specs=(fwd(W), bwd(W)),
        out_shape=(jax.ShapeDtypeStruct((B, S, W), BF16), jax.ShapeDtypeStruct((B, S, W), BF16)),
        scratch_shapes=[pltpu.VMEM((n_chains, MLSTM_HEAD_DIM, AUG), F32),
                        pltpu.VMEM((n_chains, 1, LANES), F32)],
        compiler_params=pltpu.CompilerParams(dimension_semantics=("arbitrary", "arbitrary"),
                                             vmem_limit_bytes=VMEM_LIMIT_BYTES),
        name="mlstm",
    )(qkv3, g3, qkv3, g3)


def _merge_kernel(x_ref, ya_ref, hf_ref, hb_ref, mo_ref, gates_ref, mnw_ref, wa_ref, wm_ref,
                  wo_ref, pmw_ref, pfw_ref, x1_ref, h2_ref):
    dh = MLSTM_HEAD_DIM
    tm = x_ref.shape[0]
    n_sub = MERGE_SUBTILES if tm % (MERGE_SUBTILES * 16) == 0 else 1
    subs = [pl.ds(r * (tm // n_sub), tm // n_sub) for r in range(n_sub)]
    st = [dict() for _ in subs]

    def stage_recurrent(i):
        rows, rec = subs[i], None
        for h in range(N_MLSTM_HEADS):
            hs = slice(h * dh, (h + 1) * dh)
            o = (jax.nn.sigmoid(mo_ref[rows, hs].astype(F32))
                 * (hf_ref[rows, hs].astype(F32) + hb_ref[rows, hs].astype(F32)))
            part = jnp.dot(_rms(o, mnw_ref[:, hs]).astype(BF16), wm_ref[hs, :],
                           preferred_element_type=F32)
            rec = part if rec is None else rec + part
        st[i]["rec"] = rec

    def stage_attention(i):
        rows = subs[i]
        st[i]["att"] = gates_ref[rows, :D_MODEL].astype(F32) * jnp.dot(
            ya_ref[rows, :], wa_ref[...], preferred_element_type=F32)

    def stage_mix(i):
        rows = subs[i]
        merged = st[i]["att"] + gates_ref[rows, D_MODEL:].astype(F32) * st[i]["rec"]
        st[i]["mixed"] = jnp.dot(merged.astype(BF16), wo_ref[...], preferred_element_type=F32)

    def stage_norms(i):
        rows = subs[i]
        x1 = x_ref[rows, :] + _rms(st[i]["mixed"], pmw_ref[...])
        x1_ref[rows, :] = x1
        h2_ref[rows, :] = _rms(x1, pfw_ref[...]).astype(BF16)

    _emit_staggered((stage_recurrent, stage_attention, stage_mix, stage_norms), n_sub)


def _merge(x2d, ya, hf, hb, mo, gates, mnw, wa, wm, wo, pmw, pfw):
    T = x2d.shape[0]
    tm = min(ROW_TILE, T)
    assert T % tm == 0
    row = pl.BlockSpec((tm, D_MODEL), lambda i: (i, 0))
    sq = _const_spec((D_MODEL, D_MODEL))
    vec = _const_spec((1, D_MODEL))
    return pl.pallas_call(
        _merge_kernel,
        grid=(T // tm,),
        in_specs=[row, row, row, row, row, pl.BlockSpec((tm, 2 * D_MODEL), lambda i: (i, 0)),
                  vec, sq, sq, sq, vec, vec],
        out_specs=(row, row),
        out_shape=(jax.ShapeDtypeStruct((T, D_MODEL), F32),
                   jax.ShapeDtypeStruct((T, D_MODEL), BF16)),
        compiler_params=pltpu.CompilerParams(dimension_semantics=("parallel",),
                                             vmem_limit_bytes=VMEM_LIMIT_BYTES),
        name="merge",
    )(x2d, ya, hf, hb, mo, gates, mnw, wa, wm, wo, pmw, pfw)


HALO = 16


def _ffn_kernel(x1_ref, h2_ref, hp_ref, hn_ref, wu_ref, cw_ref, cb_ref, wd_ref, nw_ref,
                out_ref, f_s, *, tiles_per_seq):
    i = pl.program_id(0)
    tm = h2_ref.shape[0]
    has_prev = (i % tiles_per_seq) != 0
    has_next = (i % tiles_per_seq) != tiles_per_seq - 1
    h2 = h2_ref[...]
    h2_ext = jnp.concatenate([hp_ref[...], h2, hn_ref[...]], axis=0)
    row = lax.broadcasted_iota(jnp.int32, (tm, 1), 0)
    n_chunks = D_FF // FF_CHUNK
    st = [dict() for _ in range(n_chunks)]

    def stage_up(c):
        cs = slice(c * FF_CHUNK, (c + 1) * FF_CHUNK)
        gs = slice(D_FF + c * FF_CHUNK, D_FF + (c + 1) * FF_CHUNK)
        st[c]["a_ext"] = jnp.dot(h2_ext, wu_ref[:, cs], preferred_element_type=F32)
        st[c]["gate"] = jnp.dot(h2, wu_ref[:, gs], preferred_element_type=F32)

    def stage_act(c):
        cs = slice(c * FF_CHUNK, (c + 1) * FF_CHUNK)
        a_ext = st[c]["a_ext"]
        a = a_ext[HALO:HALO + tm]
        a_prev = jnp.where(has_prev, a_ext[HALO - 1:HALO], 0.0)
        a_next = jnp.where(has_next, a_ext[HALO + tm:HALO + tm + 1], 0.0)
        a_dn = jnp.where(row == 0, a_prev, pltpu.roll(a, 1, axis=0))
        a_up = jnp.where(row == tm - 1, a_next, pltpu.roll(a, tm - 1, axis=0))
        conv = (a_dn * cw_ref[0:1, cs] + a * cw_ref[1:2, cs] + a_up * cw_ref[2:3, cs]
                + cb_ref[:, cs])
        f_s[:, cs] = (jax.nn.gelu(conv, approximate=True) * st[c]["gate"]).astype(BF16)

    _emit_staggered((stage_up, stage_act), n_chunks)
    n_sub = FFN_DOWN_SUBTILES if tm % (FFN_DOWN_SUBTILES * 16) == 0 else 1
    subs = [pl.ds(r * (tm // n_sub), tm // n_sub) for r in range(n_sub)]
    ys = [jnp.dot(f_s[rows, :], wd_ref[...], preferred_element_type=F32) for rows in subs]
    for rows, y in zip(subs, ys):
        out_ref[rows, :] = x1_ref[rows, :] + _rms(y, nw_ref[...])


def _ffn(x1, h2, S, wu, cw, cb, wd, nw):
    T = x1.shape[0]
    tm = min(FFN_ROW_TILE, S)
    assert S % tm == 0 and tm % HALO == 0 and D_FF % FF_CHUNK == 0
    nt = T // tm
    hb = tm // HALO
    row = lambda dt: pl.BlockSpec((tm, D_MODEL), lambda i: (i, 0))
    return pl.pallas_call(
        functools.partial(_ffn_kernel, tiles_per_seq=S // tm),
        grid=(nt,),
        in_specs=[row(F32), row(BF16),
                  pl.BlockSpec((HALO, D_MODEL), lambda i: (jnp.maximum(i * hb - 1, 0), 0)),
                  pl.BlockSpec((HALO, D_MODEL),
                               lambda i: (jnp.minimum((i + 1) * hb, nt * hb - 1), 0)),
                  _const_spec((D_MODEL, 2 * D_FF)), _const_spec((3, D_FF)),
                  _const_spec((1, D_FF)), _const_spec((D_FF, D_MODEL)),
                  _const_spec((1, D_MODEL))],
        out_specs=row(F32),
        out_shape=jax.ShapeDtypeStruct((T, D_MODEL), F32),
        scratch_shapes=[pltpu.VMEM((tm, D_FF), BF16)],
        compiler_params=pltpu.CompilerParams(dimension_semantics=("parallel",),
                                             vmem_limit_bytes=VMEM_LIMIT_BYTES),
        name="ffn",
    )(x1, h2, h2, h2, wu, cw, cb, wd, nw)


def _pack_tail(t):
    n_gate = IN_WIDTH - HEAD_WIDTH - 2 * D_MODEL
    pad = jnp.zeros(t.shape[:-1] + (GATE_SLAB - n_gate,), t.dtype)
    return jnp.concatenate([t[..., HEAD_WIDTH:HEAD_WIDTH + n_gate], pad,
                            t[..., HEAD_WIDTH + n_gate:]], axis=-1)


def _encoder_layer(x, p):
    B, S, _ = x.shape
    T = B * S
    x2d = x.reshape(T, D_MODEL)
    aq, ak, av, mqkv, mo, gl, gates = _inproj(x2d, p["pre_mix_norm"], p["w_in"], p["b_in"],
                                              p["w_in_tail"], p["b_in_tail"])
    ya = _attention(aq, ak, av, p["attn_sink"], B, S).reshape(T, D_MODEL)
    seq = lambda t: t.reshape(B, S, t.shape[-1])
    hf, hb = _mlstm(seq(mqkv), seq(gl))
    x1, h2 = _merge(x2d, ya, hf.reshape(T, D_MODEL), hb.reshape(T, D_MODEL), mo, gates,
                    p["mlstm_norm"], p["w_att_branch"], p["w_mlstm_branch"], p["w_out"],
                    p["post_mix_norm"], p["pre_ffn_norm"])
    y = _ffn(x1, h2, S, p["w_ffn_up"], p["ffn_conv_w"], p["ffn_conv_b"], p["w_ffn_down"],
             p["post_ffn_norm"])
    return y.reshape(B, S, D_MODEL)


def kernel(x_prompt, x_sample, pre_mix_norm, w_in, b_in, attn_sink, mlstm_norm, w_att_branch,
           w_mlstm_branch, w_out, post_mix_norm, pre_ffn_norm, w_ffn_up, ffn_conv_w, ffn_conv_b,
           w_ffn_down, post_ffn_norm):
    depth = w_in.shape[0]
    layers = []
    for l in range(depth):
        assert w_in.shape[-1] == IN_WIDTH
        w_bf = w_in[l].astype(BF16)
        layers.append(dict(
            pre_mix_norm=pre_mix_norm[l][None, :], w_in=w_bf, b_in=b_in[l][None, :],
            w_in_tail=_pack_tail(w_bf), b_in_tail=_pack_tail(b_in[l])[None, :],
            attn_sink=attn_sink[l], mlstm_norm=mlstm_norm[l][None, :],
            w_att_branch=w_att_branch[l].astype(BF16),
            w_mlstm_branch=w_mlstm_branch[l].astype(BF16), w_out=w_out[l].astype(BF16),
            post_mix_norm=post_mix_norm[l][None, :], pre_ffn_norm=pre_ffn_norm[l][None, :],
            w_ffn_up=w_ffn_up[l].astype(BF16), ffn_conv_w=ffn_conv_w[l],
            ffn_conv_b=ffn_conv_b[l][None, :], w_ffn_down=w_ffn_down[l].astype(BF16),
            post_ffn_norm=post_ffn_norm[l][None, :]))

    def trunk(x):
        for p in layers:
            x = _encoder_layer(x, p)
        return x

    return (trunk(x_prompt), trunk(x_sample))
```
